```python
import math
import jax, jax.numpy as jnp
from jax import lax
import numpy as np

D_MODEL = 2048
BATCH = 1
SEQ = 16384
DEPTH = 1
DEC_BATCH = 8
DEC_SEQ = 2048
PAST_LEN = 128

F_WIDTH = D_MODEL // 2
N_FG = 8
FG_DIM = F_WIDTH // N_FG
A_WIDTH = D_MODEL - F_WIDTH
N_HEADS = 8
HEAD_DIM = A_WIDTH // (2 * N_HEADS)
V_DIM = 2 * HEAD_DIM
QK_WIDTH = N_HEADS * 2 * HEAD_DIM
V_WIDTH = N_HEADS * V_DIM
IN_WIDTH = F_WIDTH + 2 * QK_WIDTH + V_WIDTH
ROPE_THETA = 10000.0
Q_BLOCK = 128
N_EXPERTS = 32
TOP_K = 4
D_FF = D_MODEL
SWIGLU_ALPHA = 1.702
SWIGLU_LIMIT = 7.0
MOE_BLOCK = 256
EPS = 1e-5

kernel_name = 'fnet_diffattn_moe_hybrid_encoder'


def rmsnorm(x, g):
    xf = x.astype(jnp.float32)
    y = xf * lax.rsqrt(jnp.mean(xf * xf, axis=-1, keepdims=True) + EPS)
    return (y * g.astype(jnp.float32)).astype(x.dtype)


def rope_tables(S):
    inv = 1.0 / (ROPE_THETA ** (jnp.arange(0, HEAD_DIM, 2, dtype=jnp.float32) / HEAD_DIM))
    ang = jnp.arange(S, dtype=jnp.float32)[:, None] * inv[None, :]
    ang = jnp.concatenate([ang, ang], axis=-1)
    return jnp.cos(ang), jnp.sin(ang)


def apply_rope(x, cos, sin):
    xf = x.astype(jnp.float32)
    x1, x2 = jnp.split(xf, 2, axis=-1)
    rot = jnp.concatenate([-x2, x1], axis=-1)
    c = cos[None, :, None, None, :]
    s = sin[None, :, None, None, :]
    return (xf * c + rot * s).astype(x.dtype)


def fourier_mixer(u, w_f, b_f):
    B, S, _ = u.shape
    ug = u.reshape(B, S, N_FG, FG_DIM).astype(jnp.float32)
    ug = jnp.fft.fft2(ug, axes=(1, 3), norm='ortho').real.astype(u.dtype)
    out = jnp.einsum('bsgc,gcd->bsgd', ug, w_f) + b_f
    return out.reshape(B, S, F_WIDTH)


def diff_attention(q, k, v, lam, lam_init, sub_g):
    B, S = q.shape[0], q.shape[1]
    nq = S // Q_BLOCK
    scale = 1.0 / math.sqrt(HEAD_DIM)
    qb = jnp.moveaxis(q.reshape(B, nq, Q_BLOCK, N_HEADS, 2, HEAD_DIM), 1, 0)

    def one_block(qblk):
        s = jnp.einsum('bqhcd,bkhcd->bhcqk', qblk, k).astype(jnp.float32) * scale
        p = jax.nn.softmax(s, axis=-1)
        a = (p[:, :, 0] - lam * p[:, :, 1]).astype(v.dtype)
        return jnp.einsum('bhqk,bkhv->bqhv', a, v)

    o = lax.map(one_block, qb)
    o = jnp.moveaxis(o, 0, 1).reshape(B, S, N_HEADS, V_DIM)
    o = rmsnorm(o, sub_g) * jnp.asarray(1.0 - lam_init, o.dtype)
    return o.reshape(B, S, A_WIDTH)


def clamped_swiglu(h):
    x_glu = h[..., ::2]
    x_lin = h[..., 1::2]
    x_glu = jnp.minimum(x_glu, SWIGLU_LIMIT)
    x_lin = jnp.clip(x_lin, -SWIGLU_LIMIT, SWIGLU_LIMIT)
    return x_glu * jax.nn.sigmoid(SWIGLU_ALPHA * x_glu) * (x_lin + 1.0)


def moe_ffn(x2d, w_router, b_router, w1, b1, w2, b2):
    T = x2d.shape[0]
    M = T * TOP_K
    n_blocks = -(-M // MOE_BLOCK) + N_EXPERTS
    P = n_blocks * MOE_BLOCK
    logits = (x2d @ w_router).astype(jnp.float32) + b_router.astype(jnp.float32)
    top_vals, top_idx = lax.top_k(logits, TOP_K)
    gates = jax.nn.softmax(top_vals, axis=-1)
    flat_e = top_idx.reshape(-1).astype(jnp.int32)
    flat_tok = jnp.repeat(jnp.arange(T, dtype=jnp.int32), TOP_K)
    flat_g = gates.reshape(-1)
    order = jnp.argsort(flat_e, stable=True)
    sorted_e = flat_e[order]
    counts = jnp.bincount(flat_e, length=N_EXPERTS).astype(jnp.int32)
    start = jnp.cumsum(counts) - counts
    padded = (counts + MOE_BLOCK - 1) // MOE_BLOCK * MOE_BLOCK
    padded_end = jnp.cumsum(padded)
    padded_start = padded_end - padded
    dest = padded_start[sorted_e] + jnp.arange(M, dtype=jnp.int32) - start[sorted_e]
    buf_tok = jnp.zeros((P,), jnp.int32).at[dest].set(flat_tok[order])
    buf_gate = jnp.zeros((P,), jnp.float32).at[dest].set(flat_g[order])
    block_start = jnp.arange(n_blocks, dtype=jnp.int32) * MOE_BLOCK
    block_e = jnp.minimum(jnp.searchsorted(padded_end, block_start, side='right'),
                          N_EXPERTS - 1).astype(jnp.int32)

    def expert_block(args):
        tok, gate, e = args
        xb = x2d[tok]
        h = xb @ w1[e] + b1[e]
        y = clamped_swiglu(h) @ w2[e] + b2[e]
        return y * gate.astype(y.dtype)[:, None]

    ys = lax.map(expert_block, (buf_tok.reshape(n_blocks, MOE_BLOCK),
                                buf_gate.reshape(n_blocks, MOE_BLOCK), block_e))
    out = jnp.zeros((T, x2d.shape[1]), ys.dtype).at[buf_tok].add(ys.reshape(P, -1))
    return out.astype(x2d.dtype)


def encoder_layer(x, l, norm1_g, w_in, q_norm_g, k_norm_g, lambda_q1, lambda_k1,
                  lambda_q2, lambda_k2, sub_norm_g, w_fourier, b_fourier, w_out,
                  norm2_g, w_router, b_router, w1, b1, w2, b2):
    B, S, D = x.shape
    lam_init = 0.8 - 0.6 * math.exp(-0.3 * l)
    h = rmsnorm(x, norm1_g)
    proj = h @ w_in
    u_f = proj[..., :F_WIDTH]
    q = proj[..., F_WIDTH:F_WIDTH + QK_WIDTH].reshape(B, S, N_HEADS, 2, HEAD_DIM)
    k = proj[..., F_WIDTH + QK_WIDTH:F_WIDTH + 2 * QK_WIDTH].reshape(B, S, N_HEADS, 2, HEAD_DIM)
    v = proj[..., F_WIDTH + 2 * QK_WIDTH:].reshape(B, S, N_HEADS, V_DIM)
    f_out = fourier_mixer(u_f, w_fourier, b_fourier)
    cos, sin = rope_tables(S)
    q = apply_rope(rmsnorm(q, q_norm_g), cos, sin)
    k = apply_rope(rmsnorm(k, k_norm_g), cos, sin)
    lam = (jnp.exp(jnp.sum(lambda_q1.astype(jnp.float32) * lambda_k1.astype(jnp.float32)))
           - jnp.exp(jnp.sum(lambda_q2.astype(jnp.float32) * lambda_k2.astype(jnp.float32)))
           + lam_init)
    a_out = diff_attention(q, k, v, lam, lam_init, sub_norm_g)
    x = x + jnp.concatenate([f_out, a_out], axis=-1) @ w_out
    h2 = rmsnorm(x, norm2_g).reshape(B * S, D)
    x = x + moe_ffn(h2, w_router, b_router, w1, b1, w2, b2).reshape(B, S, D)
    return x


def setup_inputs(seed: int = 0) -> dict:
    key = jax.random.key(seed)
    ks = jax.random.split(key, 24)
    f32 = jnp.float32
    nrm = lambda k, shape, s: jax.random.normal(k, shape, f32) * s
    return {
        'x_prompt': nrm(ks[0], (BATCH, SEQ, D_MODEL), 1.0),
        'x_sample': nrm(ks[1], (DEC_BATCH, DEC_SEQ, D_MODEL), 1.0),
        'norm1_g': 1.0 + nrm(ks[2], (DEPTH, D_MODEL), 0.02),
        'w_in': nrm(ks[3], (DEPTH, D_MODEL, IN_WIDTH), D_MODEL ** -0.5),
        'q_norm_g': 1.0 + nrm(ks[4], (DEPTH, HEAD_DIM), 0.02),
        'k_norm_g': 1.0 + nrm(ks[5], (DEPTH, HEAD_DIM), 0.02),
        'lambda_q1': nrm(ks[6], (DEPTH, HEAD_DIM), 0.1),
        'lambda_k1': nrm(ks[7], (DEPTH, HEAD_DIM), 0.1),
        'lambda_q2': nrm(ks[8], (DEPTH, HEAD_DIM), 0.1),
        'lambda_k2': nrm(ks[9], (DEPTH, HEAD_DIM), 0.1),
        'sub_norm_g': 1.0 + nrm(ks[10], (DEPTH, V_DIM), 0.02),
        'w_fourier': nrm(ks[11], (DEPTH, N_FG, FG_DIM, FG_DIM), FG_DIM ** -0.5),
        'b_fourier': nrm(ks[12], (DEPTH, N_FG, FG_DIM), 0.02),
        'w_out': nrm(ks[13], (DEPTH, D_MODEL, D_MODEL), D_MODEL ** -0.5),
        'norm2_g': 1.0 + nrm(ks[14], (DEPTH, D_MODEL), 0.02),
        'w_router': nrm(ks[15], (DEPTH, D_MODEL, N_EXPERTS), D_MODEL ** -0.5),
        'b_router': nrm(ks[16], (DEPTH, N_EXPERTS), 0.01),
        'w1': nrm(ks[17], (DEPTH, N_EXPERTS, D_MODEL, 2 * D_FF), D_MODEL ** -0.5),
        'b1': nrm(ks[18], (DEPTH, N_EXPERTS, 2 * D_FF), 0.02),
        'w2': nrm(ks[19], (DEPTH, N_EXPERTS, D_FF, D_MODEL), D_FF ** -0.5),
        'b2': nrm(ks[20], (DEPTH, N_EXPERTS, D_MODEL), 0.02),
    }


def reference(x_prompt, x_sample, norm1_g, w_in, q_norm_g, k_norm_g, lambda_q1,
              lambda_k1, lambda_q2, lambda_k2, sub_norm_g, w_fourier, b_fourier,
              w_out, norm2_g, w_router, b_router, w1, b1, w2, b2):
    y_prompt = x_prompt
    y_sample = x_sample
    for l in range(DEPTH):
        p = (norm1_g[l], w_in[l], q_norm_g[l], k_norm_g[l], lambda_q1[l], lambda_k1[l],
             lambda_q2[l], lambda_k2[l], sub_norm_g[l], w_fourier[l], b_fourier[l],
             w_out[l], norm2_g[l], w_router[l], b_router[l], w1[l], b1[l], w2[l], b2[l])
        y_prompt = encoder_layer(y_prompt, l, *p)
        y_sample = encoder_layer(y_sample, l, *p)
    return (y_prompt, y_sample)
```

```python
import functools
import math

import jax
import jax.numpy as jnp
from jax import lax
from jax.experimental import pallas as pl
from jax.experimental.pallas import tpu as pltpu

F32 = jnp.float32
BF16 = jnp.bfloat16

D_MODEL = 2048
F_WIDTH = 1024
N_FG = 8
FG_DIM = 128
N_HEADS = 8
HEAD_DIM = 64
V_DIM = 128
QK_WIDTH = 1024
N_EXPERTS = 32
TOP_K = 4
D_FF = 2048
ROPE_THETA = 10000.0
SWIGLU_ALPHA = 1.702
SWIGLU_LIMIT = 7.0
EPS = 1e-5

LANES = 128
ROW_TILES = D_MODEL // LANES
DFT_N2 = 128
VMEM_LIMIT = 48 * 1024 * 1024

MOE_TM = 512
MOE_TF = 512


def _cparams(sem):
    return pltpu.CompilerParams(dimension_semantics=sem, vmem_limit_bytes=VMEM_LIMIT)


def _norm_rope_store(acc, g_ref, cos_ref, sin_ref, o_ref, scale):
    tm = acc.shape[0]
    r = lax.broadcasted_iota(jnp.int32, (LANES, LANES), 0) // HEAD_DIM
    c = lax.broadcasted_iota(jnp.int32, (LANES, LANES), 1) // HEAD_DIM
    group_mean = jnp.where(r == c, 1.0 / HEAD_DIM, 0.0).astype(BF16)
    lane = lax.broadcasted_iota(jnp.int32, (tm, LANES), 1)
    first_half = (lane % HEAD_DIM) < (HEAD_DIM // 2)
    cos = cos_ref[...]
    sin = sin_ref[...]
    g = g_ref[...]
    for h in range(acc.shape[1] // LANES):
        a = acc[:, h * LANES:(h + 1) * LANES]
        sq = a * a
        hi = sq.astype(BF16)
        lo = (sq - hi.astype(F32)).astype(BF16)
        ms = (jnp.dot(hi, group_mean, preferred_element_type=F32)
              + jnp.dot(lo, group_mean, preferred_element_type=F32))
        y = a * lax.rsqrt(ms + EPS) * g
        rot = jnp.where(first_half, -pltpu.roll(y, LANES - HEAD_DIM // 2, 1),
                        pltpu.roll(y, HEAD_DIM // 2, 1))
        o_ref[:, h * LANES:(h + 1) * LANES] = ((y * cos + rot * sin) * scale).astype(BF16)


def _inproj_kernel(x_ref, g1_ref, w_ref, cos_ref, sin_ref, gq_ref, gk_ref,
                   u_ref, q_ref, k_ref, v_ref, h_ref):
    j = pl.program_id(1)

    @pl.when(j == 0)
    def _():
        x = x_ref[...]
        ms = jnp.mean(x * x, axis=-1, keepdims=True)
        h_ref[...] = (x * lax.rsqrt(ms + EPS) * g1_ref[...]).astype(BF16)

    acc = jnp.dot(h_ref[...], w_ref[...], preferred_element_type=F32)

    @pl.when(j == 0)
    def _():
        u_ref[...] = acc.astype(BF16)

    @pl.when(j == 1)
    def _():
        _norm_rope_store(acc, gq_ref, cos_ref, sin_ref, q_ref, 1.0 / math.sqrt(HEAD_DIM))

    @pl.when(j == 2)
    def _():
        _norm_rope_store(acc, gk_ref, cos_ref, sin_ref, k_ref, 1.0)

    @pl.when(j == 3)
    def _():
        v_ref[...] = acc.astype(BF16)


def _inproj(x, g1, w_in_b, cos, sin, gq, gk, *, tm, n_p, n_s):
    T = x.shape[0]

    def pos_blk(i, j):
        return (jnp.where(i < n_p, i, (i - n_p) % n_s), 0)

    out = jax.ShapeDtypeStruct((T, 1024), BF16)
    return pl.pallas_call(
        _inproj_kernel,
        grid=(T // tm, 4),
        in_specs=[
            pl.BlockSpec((tm, D_MODEL), lambda i, j: (i, 0)),
            pl.BlockSpec((1, D_MODEL), lambda i, j: (0, 0)),
            pl.BlockSpec((D_MODEL, 1024), lambda i, j: (0, j)),
            pl.BlockSpec((tm, LANES), pos_blk),
            pl.BlockSpec((tm, LANES), pos_blk),
            pl.BlockSpec((1, LANES), lambda i, j: (0, 0)),
            pl.BlockSpec((1, LANES), lambda i, j: (0, 0)),
        ],
        out_specs=[pl.BlockSpec((tm, 1024), lambda i, j: (i, 0))] * 4,
        out_shape=[out] * 4,
        scratch_shapes=[pltpu.VMEM((tm, D_MODEL), BF16)],
        compiler_params=_cparams(("parallel", "arbitrary")),
        name="inproj",
    )(x, g1, w_in_b, cos, sin, gq, gk)


def _fft1_kernel(x_ref, c1_ref, s1_ref, twc_ref, tws_ref, tr_ref, ti_ref, *, tb):
    x = x_ref[...]
    zr = jnp.dot(c1_ref[...], x, preferred_element_type=F32)
    zi = -jnp.dot(s1_ref[...], x, preferred_element_type=F32)
    for t in range(tb):
        c = jnp.concatenate([twc_ref[t]] * N_FG, axis=1)
        s = jnp.concatenate([tws_ref[t]] * N_FG, axis=1)
        a = zr[:, t * F_WIDTH:(t + 1) * F_WIDTH]
        b = zi[:, t * F_WIDTH:(t + 1) * F_WIDTH]
        tr_ref[t] = (a * c + b * s).astype(BF16)
        ti_ref[t] = (b * c - a * s).astype(BF16)


def _fft2_kernel(tr_ref, ti_ref, c2_ref, s2_ref, cc_ref, sc_ref, wf_ref, bf_ref, o_ref, *, scale):
    tr = tr_ref[...]
    ti = ti_ref[...]
    c2 = c2_ref[...]
    s2 = s2_ref[...]
    ur = (jnp.dot(c2, tr, preferred_element_type=F32) + jnp.dot(s2, ti, preferred_element_type=F32))
    ui = (jnp.dot(c2, ti, preferred_element_type=F32) - jnp.dot(s2, tr, preferred_element_type=F32))
    cc = cc_ref[...]
    sc = sc_ref[...]
    for c in range(tr.shape[1] // FG_DIM):
        g = c % N_FG
        sl = slice(c * FG_DIM, (c + 1) * FG_DIM)
        y = (jnp.dot(ur[:, sl].astype(BF16), cc, preferred_element_type=F32)
             + jnp.dot(ui[:, sl].astype(BF16), sc, preferred_element_type=F32))
        y = (y * scale).astype(BF16)
        o_ref[:, sl] = (jnp.dot(y, wf_ref[g], preferred_element_type=F32) + bf_ref[g]).astype(BF16)


def _dft_mats(n):
    j = jnp.arange(n, dtype=jnp.int32)
    ang = (2.0 * math.pi / n) * ((j[:, None] * j[None, :]) % n).astype(F32)
    return jnp.cos(ang).astype(BF16), jnp.sin(ang).astype(BF16)


def _fourier(u, w_f_b, b_f, *, n_seq, seq_len):
    n1 = seq_len // DFT_N2
    cols = DFT_N2 * F_WIDTH
    x = u.reshape(n_seq, n1, cols)
    c1, s1 = _dft_mats(n1)
    c2, s2 = _dft_mats(DFT_N2)
    k1 = jnp.arange(n1, dtype=jnp.int32)
    sv = jnp.arange(DFT_N2, dtype=jnp.int32)
    ang = (2.0 * math.pi / seq_len) * (sv[:, None] * k1[None, :]).astype(F32)
    twc = jnp.broadcast_to(jnp.cos(ang)[:, :, None], (DFT_N2, n1, LANES))
    tws = jnp.broadcast_to(jnp.sin(ang)[:, :, None], (DFT_N2, n1, LANES))

    tb = 8
    t_shape = jax.ShapeDtypeStruct((n_seq, DFT_N2, n1, F_WIDTH), BF16)
    tr, ti = pl.pallas_call(
        functools.partial(_fft1_kernel, tb=tb),
        grid=(n_seq, DFT_N2 // tb),
        in_specs=[
            pl.BlockSpec((None, n1, tb * F_WIDTH), lambda b, j: (b, 0, j)),
            pl.BlockSpec((n1, n1), lambda b, j: (0, 0)),
            pl.BlockSpec((n1, n1), lambda b, j: (0, 0)),
            pl.BlockSpec((tb, n1, LANES), lambda b, j: (j, 0, 0)),
            pl.BlockSpec((tb, n1, LANES), lambda b, j: (j, 0, 0)),
        ],
        out_specs=[pl.BlockSpec((None, tb, n1, F_WIDTH), lambda b, j: (b, j, 0, 0))] * 2,
        out_shape=[t_shape] * 2,
        compiler_params=_cparams(("parallel", "parallel")),
        name="fft_stage1",
    )(x, c1, s1, twc, tws)

    tn = min(4096, n1 * F_WIDTH)
    tr = tr.reshape(n_seq, DFT_N2, n1 * F_WIDTH)
    ti = ti.reshape(n_seq, DFT_N2, n1 * F_WIDTH)
    scale = 1.0 / math.sqrt(seq_len * FG_DIM)
    const = lambda b, j: (0, 0)
    f = pl.pallas_call(
        functools.partial(_fft2_kernel, scale=scale),
        grid=(n_seq, n1 * F_WIDTH // tn),
        in_specs=[
            pl.BlockSpec((None, DFT_N2, tn), lambda b, j: (b, 0, j)),
            pl.BlockSpec((None, DFT_N2, tn), lambda b, j: (b, 0, j)),
            pl.BlockSpec((DFT_N2, DFT_N2), const),
            pl.BlockSpec((DFT_N2, DFT_N2), const),
            pl.BlockSpec((FG_DIM, FG_DIM), const),
            pl.BlockSpec((FG_DIM, FG_DIM), const),
            pl.BlockSpec((N_FG, FG_DIM, FG_DIM), lambda b, j: (0, 0, 0)),
            pl.BlockSpec((N_FG, 1, FG_DIM), lambda b, j: (0, 0, 0)),
        ],
        out_specs=pl.BlockSpec((None, DFT_N2, tn), lambda b, j: (b, 0, j)),
        out_shape=jax.ShapeDtypeStruct((n_seq, DFT_N2, n1 * F_WIDTH), BF16),
        compiler_params=_cparams(("parallel", "parallel")),
        name="fft_stage2",
    )(tr, ti, c2, s2, c2, s2, w_f_b, b_f.reshape(N_FG, 1, FG_DIM))
    return f.reshape(n_seq * seq_len, F_WIDTH)


def _attn_kernel(lam_ref, subg_ref, q_ref, k_ref, v_ref, o_ref,
                 acc0, acc1, m0, m1, l0, l1, *, seq_len, tk, lam_init):
    q = q_ref[...]
    lane = lax.broadcasted_iota(jnp.int32, q.shape, 1)
    zero = jnp.zeros_like(q)
    qs = (jnp.where(lane < HEAD_DIM, q, zero), jnp.where(lane >= HEAD_DIM, q, zero))
    accs, ms, ls = (acc0, acc1), (m0, m1), (l0, l1)
    for c in range(2):
        accs[c][...] = jnp.zeros_like(accs[c])
        ms[c][...] = jnp.full_like(ms[c], -jnp.inf)
        ls[c][...] = jnp.zeros_like(ls[c])

    def body(j, carry):
        start = pl.multiple_of(j * tk, tk)
        kt = k_ref[pl.ds(start, tk), :]
        vt = v_ref[pl.ds(start, tk), :]
        for c in range(2):
            s = lax.dot_general(qs[c], kt, (((1,), (1,)), ((), ())), preferred_element_type=F32)
            m_prev = ms[c][...]
            m_new = jnp.maximum(m_prev, jnp.max(s, axis=1, keepdims=True))
            alpha = jnp.exp(m_prev - m_new)
            p = jnp.exp(s - m_new)
            ls[c][...] = alpha * ls[c][...] + jnp.sum(p, axis=1, keepdims=True)
            accs[c][...] = alpha * accs[c][...] + jnp.dot(p.astype(BF16), vt,
                                                          preferred_element_type=F32)
            ms[c][...] = m_new
        return carry

    lax.fori_loop(0, seq_len // tk, body, 0)

    lam_p = lam_ref[...]
    lam = (jnp.exp(jnp.sum(lam_p[0:1] * lam_p[1:2], axis=1, keepdims=True))
           - jnp.exp(jnp.sum(lam_p[2:3] * lam_p[3:4], axis=1, keepdims=True)) + lam_init)
    o = acc0[...] / l0[...] - lam * (acc1[...] / l1[...])
    ms_o = jnp.mean(o * o, axis=-1, keepdims=True)
    o = o * lax.rsqrt(ms_o + EPS) * subg_ref[...]
    o_ref[...] = (o * (1.0 - lam_init)).astype(BF16)


def _attention(q, k, v, lam_p, sub_g, *, row0, n_seq, seq_len, lam_init, tq, tk):
    nq = seq_len // tq
    q0 = row0 // tq
    kv0 = row0 // seq_len
    kernel = functools.partial(_attn_kernel, seq_len=seq_len, tk=tk, lam_init=lam_init)
    return pl.pallas_call(
        kernel,
        grid=(n_seq, N_HEADS, nq),
        in_specs=[
            pl.BlockSpec((4, HEAD_DIM), lambda b, h, i: (0, 0)),
            pl.BlockSpec((1, V_DIM), lambda b, h, i: (0, 0)),
            pl.BlockSpec((tq, LANES), lambda b, h, i: (q0 + b * nq + i, h)),
            pl.BlockSpec((seq_len, LANES), lambda b, h, i: (kv0 + b, h)),
            pl.BlockSpec((seq_len, LANES), lambda b, h, i: (kv0 + b, h)),
        ],
        out_specs=pl.BlockSpec((tq, V_DIM), lambda b, h, i: (b * nq + i, h)),
        out_shape=jax.ShapeDtypeStruct((n_seq * seq_len, N_HEADS * V_DIM), BF16),
        scratch_shapes=[pltpu.VMEM((tq, V_DIM), F32), pltpu.VMEM((tq, V_DIM), F32),
                        pltpu.VMEM((tq, 1), F32), pltpu.VMEM((tq, 1), F32),
                        pltpu.VMEM((tq, 1), F32), pltpu.VMEM((tq, 1), F32)],
        compiler_params=_cparams(("parallel", "parallel", "arbitrary")),
        name="diff_attn",
    )(lam_p, sub_g, q, k, v)


def _outproj_kernel(f_ref, a_ref, x_ref, wo_ref, g2_ref, wr_ref, br_ref,
                    x1_ref, h2_ref, idx_ref, gexp_ref):
    acc = (jnp.dot(f_ref[...], wo_ref[0:F_WIDTH, :], preferred_element_type=F32)
           + jnp.dot(a_ref[...], wo_ref[F_WIDTH:D_MODEL, :], preferred_element_type=F32))
    x1 = x_ref[...] + acc
    x1_ref[...] = x1
    ms = jnp.mean(x1 * x1, axis=-1, keepdims=True)
    h2 = (x1 * lax.rsqrt(ms + EPS) * g2_ref[...]).astype(BF16)
    h2_ref[...] = h2
    logits = jnp.dot(h2, wr_ref[...], preferred_element_type=F32) + br_ref[...]
    tm = logits.shape[0]
    lane = lax.broadcasted_iota(jnp.int32, (tm, N_EXPERTS), 1)
    cur = logits
    vals, idxs = [], []
    for _ in range(TOP_K):
        m = jnp.max(cur, axis=1, keepdims=True)
        am = jnp.min(jnp.where(cur == m, lane, N_EXPERTS), axis=1, keepdims=True)
        vals.append(m)
        idxs.append(am)
        cur = jnp.where(lane == am, -jnp.inf, cur)
    idx_ref[...] = jnp.concatenate(idxs, axis=1)
    es = [jnp.exp(v - vals[0]) for v in vals]
    den = es[0] + es[1] + es[2] + es[3]
    gexp_ref[...] = jnp.concatenate([jnp.broadcast_to(e / den, (tm, LANES)) for e in es], axis=1)


def _outproj(f, a, x, w_out_b, g2, w_r_b, b_r, *, tm):
    T = x.shape[0]
    const = lambda i: (0, 0)
    return pl.pallas_call(
        _outproj_kernel,
        grid=(T // tm,),
        in_specs=[
            pl.BlockSpec((tm, F_WIDTH), lambda i: (i, 0)),
            pl.BlockSpec((tm, F_WIDTH), lambda i: (i, 0)),
            pl.BlockSpec((tm, D_MODEL), lambda i: (i, 0)),
            pl.BlockSpec((D_MODEL, D_MODEL), const),
            pl.BlockSpec((1, D_MODEL), const),
            pl.BlockSpec((D_MODEL, N_EXPERTS), const),
            pl.BlockSpec((1, N_EXPERTS), const),
        ],
        out_specs=[
            pl.BlockSpec((tm, D_MODEL), lambda i: (i, 0)),
            pl.BlockSpec((tm, D_MODEL), lambda i: (i, 0)),
            pl.BlockSpec((tm, TOP_K), lambda i: (i, 0)),
            pl.BlockSpec((tm, TOP_K * LANES), lambda i: (i, 0)),
        ],
        out_shape=[
            jax.ShapeDtypeStruct((T, D_MODEL), F32),
            jax.ShapeDtypeStruct((T, D_MODEL), BF16),
            jax.ShapeDtypeStruct((T, TOP_K), jnp.int32),
            jax.ShapeDtypeStruct((T, TOP_K * LANES), F32),
        ],
        compiler_params=_cparams(("parallel",)),
        name="outproj_router",
    )(f, a, x, w_out_b, g2, w_r_b, b_r)


def _route(top_idx, n_blk):
    flat_e = top_idx.reshape(-1)
    onehot = (flat_e[:, None] == jnp.arange(N_EXPERTS, dtype=jnp.int32)[None, :]).astype(jnp.int32)
    csum = jnp.cumsum(onehot, axis=0)
    counts = csum[-1]
    rank = jnp.sum(csum * onehot, axis=1) - 1
    nblk_e = (counts + MOE_TM - 1) // MOE_TM
    blk_end = jnp.cumsum(nblk_e)
    blk_start = blk_end - nblk_e
    dest = jnp.sum(onehot * (blk_start * MOE_TM)[None, :], axis=1) + rank
    b = jnp.arange(n_blk, dtype=jnp.int32)
    n_used = blk_end[-1]
    blk_src = jnp.minimum(b, n_used - 1)
    blk_e = jnp.minimum(jnp.sum((blk_end[None, :] <= blk_src[:, None]).astype(jnp.int32), axis=1),
                        N_EXPERTS - 1)
    blk_used = (b < n_used).astype(jnp.int32)
    return dest.astype(jnp.int32), blk_e.astype(jnp.int32), blk_src.astype(jnp.int32), blk_used


def _scatter_kernel(dest_ref, h_ref, xs_in_ref, xs_ref, idx_smem, idx_sem, sem, *, te):
    del xs_in_ref
    cp = pltpu.make_async_copy(dest_ref.at[0, 0], idx_smem, idx_sem)
    cp.start()
    cp.wait()

    def issue(t, carry):
        for k in range(TOP_K):
            pltpu.make_async_copy(h_ref.at[t], xs_ref.at[idx_smem[TOP_K * t + k]], sem).start()
        return carry

    lax.fori_loop(0, te, issue, 0)

    def drain(t, carry):
        for k in range(TOP_K):
            pltpu.make_async_copy(h_ref.at[0], xs_ref.at[0], sem).wait()
        return carry

    lax.fori_loop(0, te, drain, 0)


def _moe_scatter(h2, dest, n_rows, *, te):
    T = h2.shape[0]
    h3 = h2.reshape(T, ROW_TILES, LANES)
    dest3 = dest.reshape(T // te, 1, te * TOP_K)
    xs0 = jnp.zeros((n_rows, ROW_TILES, LANES), BF16)
    return pl.pallas_call(
        functools.partial(_scatter_kernel, te=te),
        grid=(T // te,),
        in_specs=[
            pl.BlockSpec((1, 1, te * TOP_K), lambda i: (i, 0, 0)),
            pl.BlockSpec((te, ROW_TILES, LANES), lambda i: (i, 0, 0)),
            pl.BlockSpec(memory_space=pl.ANY),
        ],
        out_specs=pl.BlockSpec(memory_space=pl.ANY),
        out_shape=jax.ShapeDtypeStruct((n_rows, ROW_TILES, LANES), BF16),
        scratch_shapes=[pltpu.SMEM((te * TOP_K,), jnp.int32),
                        pltpu.SemaphoreType.DMA(()), pltpu.SemaphoreType.DMA(())],
        input_output_aliases={2: 0},
        compiler_params=_cparams(("arbitrary",)),
        name="moe_scatter",
    )(dest3, h3, xs0)


def _moe_ffn_kernel(be_ref, bsrc_ref, bused_ref, x_ref, w1g_ref, w1l_ref, b1g_ref, b1l_ref,
                    w2_ref, b2_ref, o_ref):
    b = pl.program_id(0)
    f = pl.program_id(1)

    @pl.when(bused_ref[b] > 0)
    def _():
        x = x_ref[...]
        hg = jnp.dot(x, w1g_ref[0], preferred_element_type=F32) + b1g_ref[0]
        hl = jnp.dot(x, w1l_ref[0], preferred_element_type=F32) + b1l_ref[0]
        hg = jnp.minimum(hg, SWIGLU_LIMIT)
        hl = jnp.clip(hl, -SWIGLU_LIMIT, SWIGLU_LIMIT)
        act = hg * jax.nn.sigmoid(SWIGLU_ALPHA * hg) * (hl + 1.0)
        part = jnp.dot(act.astype(BF16), w2_ref[0], preferred_element_type=F32)

        @pl.when(f == 0)
        def _():
            o_ref[...] = part + b2_ref[0]

        @pl.when(f > 0)
        def _():
            o_ref[...] += part

    @pl.when(jnp.logical_and(bused_ref[b] == 0, f == 0))
    def _():
        o_ref[...] = jnp.zeros_like(o_ref)


def _moe_ffn(xs, blk_e, blk_src, blk_used, w1g, w1l, b1g, b1l, w2b, b2):
    n_rows = xs.shape[0]
    n_blk = n_rows // MOE_TM
    n_f = D_FF // MOE_TF

    def f_idx(f, used):
        return jnp.where(used > 0, f, n_f - 1)

    grid_spec = pltpu.PrefetchScalarGridSpec(
        num_scalar_prefetch=3,
        grid=(n_blk, n_f),
        in_specs=[
            pl.BlockSpec((MOE_TM, D_MODEL), lambda b, f, be, bs, bu: (bs[b], 0)),
            pl.BlockSpec((1, D_MODEL, MOE_TF), lambda b, f, be, bs, bu: (be[b], 0, f_idx(f, bu[b]))),
            pl.BlockSpec((1, D_MODEL, MOE_TF), lambda b, f, be, bs, bu: (be[b], 0, f_idx(f, bu[b]))),
            pl.BlockSpec((1, 1, MOE_TF), lambda b, f, be, bs, bu: (be[b], 0, f_idx(f, bu[b]))),
            pl.BlockSpec((1, 1, MOE_TF), lambda b, f, be, bs, bu: (be[b], 0, f_idx(f, bu[b]))),
            pl.BlockSpec((1, MOE_TF, D_MODEL), lambda b, f, be, bs, bu: (be[b], f_idx(f, bu[b]), 0)),
            pl.BlockSpec((1, 1, D_MODEL), lambda b, f, be, bs, bu: (be[b], 0, 0)),
        ],
        out_specs=pl.BlockSpec((MOE_TM, D_MODEL), lambda b, f, be, bs, bu: (b, 0)),
    )
    return pl.pallas_call(
        _moe_ffn_kernel,
        grid_spec=grid_spec,
        out_shape=jax.ShapeDtypeStruct((n_rows, D_MODEL), F32),
        compiler_params=_cparams(("arbitrary", "arbitrary")),
        name="moe_ffn",
    )(blk_e, blk_src, blk_used, xs, w1g, w1l, b1g, b1l, w2b, b2)


def _combine_kernel(dest_ref, x1_ref, gexp_ref, ys_ref, o_ref, idx_smem, gbuf, idx_sem, sem, *, tc):
    cp = pltpu.make_async_copy(dest_ref.at[0, 0], idx_smem, idx_sem)
    cp.start()
    cp.wait()

    def issue(t, carry):
        for k in range(TOP_K):
            pltpu.make_async_copy(ys_ref.at[idx_smem[TOP_K * t + k]], gbuf.at[k * tc + t], sem).start()
        return carry

    lax.fori_loop(0, tc, issue, 0)

    def drain(t, carry):
        for k in range(TOP_K):
            pltpu.make_async_copy(ys_ref.at[0], gbuf.at[0], sem).wait()
        return carry

    lax.fori_loop(0, tc, drain, 0)

    acc = x1_ref[...]
    for k in range(TOP_K):
        acc = acc + gexp_ref[:, k:k + 1, :] * gbuf[pl.ds(k * tc, tc)]
    o_ref[...] = acc


def _moe_combine(x1, gexp, dest, ys, *, tc):
    T = x1.shape[0]
    n_rows = ys.shape[0]
    out = pl.pallas_call(
        functools.partial(_combine_kernel, tc=tc),
        grid=(T // tc,),
        in_specs=[
            pl.BlockSpec((1, 1, tc * TOP_K), lambda i: (i, 0, 0)),
            pl.BlockSpec((tc, ROW_TILES, LANES), lambda i: (i, 0, 0)),
            pl.BlockSpec((tc, TOP_K, LANES), lambda i: (i, 0, 0)),
            pl.BlockSpec(memory_space=pl.ANY),
        ],
        out_specs=pl.BlockSpec((tc, ROW_TILES, LANES), lambda i: (i, 0, 0)),
        out_shape=jax.ShapeDtypeStruct((T, ROW_TILES, LANES), F32),
        scratch_shapes=[pltpu.SMEM((tc * TOP_K,), jnp.int32),
                        pltpu.VMEM((tc * TOP_K, ROW_TILES, LANES), F32),
                        pltpu.SemaphoreType.DMA(()), pltpu.SemaphoreType.DMA(())],
        compiler_params=_cparams(("arbitrary",)),
        name="moe_combine",
    )(dest.reshape(T // tc, 1, tc * TOP_K), x1.reshape(T, ROW_TILES, LANES),
      gexp.reshape(T, TOP_K, LANES), ys.reshape(n_rows, ROW_TILES, LANES))
    return out.reshape(T, D_MODEL)


def _rope_tables(n_pos):
    inv = 1.0 / (ROPE_THETA ** (jnp.arange(0, HEAD_DIM, 2, dtype=F32) / HEAD_DIM))
    ang = jnp.arange(n_pos, dtype=F32)[:, None] * inv[None, :]
    ang = jnp.concatenate([ang, ang, ang, ang], axis=-1)
    return jnp.cos(ang), jnp.sin(ang)


def _layer(x, l, s_p, n_smp, s_s, norm1_g, w_in, q_norm_g, k_norm_g, lambda_q1, lambda_k1,
           lambda_q2, lambda_k2, sub_norm_g, w_fourier, b_fourier, w_out, norm2_g,
           w_router, b_router, w1, b1, w2, b2):
    T = x.shape[0]
    lam_init = 0.8 - 0.6 * math.exp(-0.3 * l)
    tm = min(512, s_s)

    cos, sin = _rope_tables(max(s_p, s_s))
    u, q, k, v = _inproj(
        x, norm1_g.reshape(1, D_MODEL), w_in.astype(BF16), cos, sin,
        jnp.tile(q_norm_g, 2).reshape(1, LANES), jnp.tile(k_norm_g, 2).reshape(1, LANES),
        tm=tm, n_p=s_p // tm, n_s=s_s // tm)

    w_f_b = w_fourier.astype(BF16)
    f = jnp.concatenate([
        _fourier(u[:s_p], w_f_b, b_fourier, n_seq=1, seq_len=s_p),
        _fourier(u[s_p:], w_f_b, b_fourier, n_seq=n_smp, seq_len=s_s)], axis=0)

    lam_p = jnp.stack([lambda_q1, lambda_k1, lambda_q2, lambda_k2]).astype(F32)
    sub_g = sub_norm_g.reshape(1, V_DIM)
    a = jnp.concatenate([
        _attention(q, k, v, lam_p, sub_g, row0=0, n_seq=1, seq_len=s_p, lam_init=lam_init,
                   tq=min(256, s_p), tk=min(512, s_p)),
        _attention(q, k, v, lam_p, sub_g, row0=s_p, n_seq=n_smp, seq_len=s_s, lam_init=lam_init,
                   tq=min(256, s_s), tk=min(512, s_s))], axis=0)

    x1, h2, top_idx, gexp = _outproj(
        f, a, x, w_out.astype(BF16), norm2_g.reshape(1, D_MODEL), w_router.astype(BF16),
        b_router.reshape(1, N_EXPERTS), tm=tm)

    n_blk = T * TOP_K // MOE_TM + N_EXPERTS
    dest, blk_e, blk_src, blk_used = _route(top_idx, n_blk)
    xs = _moe_scatter(h2, dest, n_blk * MOE_TM, te=min(256, s_s))
    ys = _moe_ffn(
        xs.reshape(n_blk * MOE_TM, D_MODEL), blk_e, blk_src, blk_used,
        w1[:, :, 0::2].astype(BF16), w1[:, :, 1::2].astype(BF16),
        b1[:, 0::2].reshape(N_EXPERTS, 1, D_FF), b1[:, 1::2].reshape(N_EXPERTS, 1, D_FF),
        w2.astype(BF16), b2.reshape(N_EXPERTS, 1, D_MODEL))
    return _moe_combine(x1, gexp, dest, ys, tc=min(128, s_s))


def kernel(x_prompt, x_sample, norm1_g, w_in, q_norm_g, k_norm_g, lambda_q1, lambda_k1, lambda_q2,
           lambda_k2, sub_norm_g, w_fourier, b_fourier, w_out, norm2_g, w_router, b_router,
           w1, b1, w2, b2):
    b_p, s_p, d = x_prompt.shape
    n_smp, s_s, _ = x_sample.shape
    assert b_p == 1 and d == D_MODEL
    x = jnp.concatenate([x_prompt.reshape(s_p, d), x_sample.reshape(n_smp * s_s, d)], axis=0)
    for l in range(norm1_g.shape[0]):
        x = _layer(x, l, s_p, n_smp, s_s, norm1_g[l], w_in[l], q_norm_g[l], k_norm_g[l],
                   lambda_q1[l], lambda_k1[l], lambda_q2[l], lambda_k2[l], sub_norm_g[l],
                   w_fourier[l], b_fourier[l], w_out[l], norm2_g[l], w_router[l], b_router[l],
                   w1[l], b1[l], w2[l], b2[l])
    return (x[:s_p].reshape(1, s_p, d), x[s_p:].reshape(n_smp, s_s, d))
```

```python
import functools
import math

import jax
import jax.numpy as jnp
from jax import lax
from jax.experimental import pallas as pl
from jax.experimental.pallas import tpu as pltpu

F32 = jnp.float32
BF16 = jnp.bfloat16

D_MODEL = 2048
F_WIDTH = 1024
N_FG = 8
FG_DIM = 128
N_HEADS = 8
HEAD_DIM = 64
V_DIM = 128
QK_WIDTH = 1024
N_EXPERTS = 32
TOP_K = 4
D_FF = 2048
ROPE_THETA = 10000.0
SWIGLU_ALPHA = 1.702
SWIGLU_LIMIT = 7.0
EPS = 1e-5

LANES = 128
HALF_D = D_MODEL // 2
DFT_N2 = 128
VMEM_LIMIT = 48 * 1024 * 1024

MOE_TM = 512
MOE_TF = 512
ATTN_TQ = 512
ATTN_TK = 512
ATTN_UNROLL = 8
SCORE_BOUND_NO_MAX = 40.0


def _cparams(sem):
    return pltpu.CompilerParams(dimension_semantics=sem, vmem_limit_bytes=VMEM_LIMIT)


def _norm_rope_store(acc, g_ref, cos_ref, sin_ref, o_ref, scale):
    tm = acc.shape[0]
    r = lax.broadcasted_iota(jnp.int32, (LANES, LANES), 0) // HEAD_DIM
    c = lax.broadcasted_iota(jnp.int32, (LANES, LANES), 1) // HEAD_DIM
    group_mean = jnp.where(r == c, 1.0 / HEAD_DIM, 0.0).astype(BF16)
    lane = lax.broadcasted_iota(jnp.int32, (tm, LANES), 1)
    first_half = (lane % HEAD_DIM) < (HEAD_DIM // 2)
    cos = cos_ref[...]
    sin = sin_ref[...]
    g = g_ref[...]
    for h in range(acc.shape[1] // LANES):
        a = acc[:, h * LANES:(h + 1) * LANES]
        sq = a * a
        hi = sq.astype(BF16)
        lo = (sq - hi.astype(F32)).astype(BF16)
        ms = (jnp.dot(hi, group_mean, preferred_element_type=F32)
              + jnp.dot(lo, group_mean, preferred_element_type=F32))
        y = a * lax.rsqrt(ms + EPS) * g
        rot = jnp.where(first_half, -pltpu.roll(y, LANES - HEAD_DIM // 2, 1),
                        pltpu.roll(y, HEAD_DIM // 2, 1))
        o_ref[:, h * LANES:(h + 1) * LANES] = ((y * cos + rot * sin) * scale).astype(BF16)


def _inproj_kernel(x_ref, g1_ref, w_ref, cos_ref, sin_ref, gq_ref, gk_ref,
                   u_ref, q_ref, k_ref, v_ref, h_ref):
    j = pl.program_id(1)

    @pl.when(j == 0)
    def _():
        x = x_ref[...]
        ms = jnp.mean(x * x, axis=-1, keepdims=True)
        h_ref[...] = (x * lax.rsqrt(ms + EPS) * g1_ref[...]).astype(BF16)

    acc = jnp.dot(h_ref[...], w_ref[...], preferred_element_type=F32)

    @pl.when(j == 0)
    def _():
        u_ref[...] = acc.astype(BF16)

    @pl.when(j == 1)
    def _():
        _norm_rope_store(acc, gq_ref, cos_ref, sin_ref, q_ref, 1.0 / math.sqrt(HEAD_DIM))

    @pl.when(j == 2)
    def _():
        _norm_rope_store(acc, gk_ref, cos_ref, sin_ref, k_ref, 1.0)

    @pl.when(j == 3)
    def _():
        v_ref[...] = acc.astype(BF16)


def _inproj(x, g1, w_in_b, cos, sin, gq, gk, *, tm, n_p, n_s):
    T = x.shape[0]

    def pos_blk(i, j):
        return (jnp.where(i < n_p, i, (i - n_p) % n_s), 0)

    out = jax.ShapeDtypeStruct((T, 1024), BF16)
    return pl.pallas_call(
        _inproj_kernel,
        grid=(T // tm, 4),
        in_specs=[
            pl.BlockSpec((tm, D_MODEL), lambda i, j: (i, 0)),
            pl.BlockSpec((1, D_MODEL), lambda i, j: (0, 0)),
            pl.BlockSpec((D_MODEL, 1024), lambda i, j: (0, j)),
            pl.BlockSpec((tm, LANES), pos_blk),
            pl.BlockSpec((tm, LANES), pos_blk),
            pl.BlockSpec((1, LANES), lambda i, j: (0, 0)),
            pl.BlockSpec((1, LANES), lambda i, j: (0, 0)),
        ],
        out_specs=[pl.BlockSpec((tm, 1024), lambda i, j: (i, 0))] * 4,
        out_shape=[out] * 4,
        scratch_shapes=[pltpu.VMEM((tm, D_MODEL), BF16)],
        compiler_params=_cparams(("parallel", "arbitrary")),
        name="inproj",
    )(x, g1, w_in_b, cos, sin, gq, gk)


def _fft1_kernel(x_ref, c1_ref, s1_ref, twc_ref, tws_ref, tr_ref, ti_ref, *, tb):
    x = x_ref[...]
    zr = jnp.dot(c1_ref[...], x, preferred_element_type=F32)
    zi = -jnp.dot(s1_ref[...], x, preferred_element_type=F32)
    for t in range(tb):
        c = jnp.concatenate([twc_ref[t]] * N_FG, axis=1)
        s = jnp.concatenate([tws_ref[t]] * N_FG, axis=1)
        a = zr[:, t * F_WIDTH:(t + 1) * F_WIDTH]
        b = zi[:, t * F_WIDTH:(t + 1) * F_WIDTH]
        tr_ref[t] = (a * c + b * s).astype(BF16)
        ti_ref[t] = (b * c - a * s).astype(BF16)


def _fft2_kernel(tr_ref, ti_ref, c2_ref, s2_ref, cc_ref, sc_ref, wf_ref, bf_ref, o_ref, *, scale):
    tr = tr_ref[...]
    ti = ti_ref[...]
    c2 = c2_ref[...]
    s2 = s2_ref[...]
    ur = (jnp.dot(c2, tr, preferred_element_type=F32) + jnp.dot(s2, ti, preferred_element_type=F32))
    ui = (jnp.dot(c2, ti, preferred_element_type=F32) - jnp.dot(s2, tr, preferred_element_type=F32))
    cc = cc_ref[...]
    sc = sc_ref[...]
    for c in range(tr.shape[1] // FG_DIM):
        g = c % N_FG
        sl = slice(c * FG_DIM, (c + 1) * FG_DIM)
        y = (jnp.dot(ur[:, sl].astype(BF16), cc, preferred_element_type=F32)
             + jnp.dot(ui[:, sl].astype(BF16), sc, preferred_element_type=F32))
        y = (y * scale).astype(BF16)
        o_ref[:, sl] = (jnp.dot(y, wf_ref[g], preferred_element_type=F32) + bf_ref[g]).astype(BF16)


def _dft_mats(n):
    j = jnp.arange(n, dtype=jnp.int32)
    ang = (2.0 * math.pi / n) * ((j[:, None] * j[None, :]) % n).astype(F32)
    return jnp.cos(ang).astype(BF16), jnp.sin(ang).astype(BF16)


def _fourier(u, w_f_b, b_f, *, n_seq, seq_len):
    n1 = seq_len // DFT_N2
    cols = DFT_N2 * F_WIDTH
    x = u.reshape(n_seq, n1, cols)
    c1, s1 = _dft_mats(n1)
    c2, s2 = _dft_mats(DFT_N2)
    k1 = jnp.arange(n1, dtype=jnp.int32)
    sv = jnp.arange(DFT_N2, dtype=jnp.int32)
    ang = (2.0 * math.pi / seq_len) * (sv[:, None] * k1[None, :]).astype(F32)
    twc = jnp.broadcast_to(jnp.cos(ang)[:, :, None], (DFT_N2, n1, LANES))
    tws = jnp.broadcast_to(jnp.sin(ang)[:, :, None], (DFT_N2, n1, LANES))

    tb = 8
    t_shape = jax.ShapeDtypeStruct((n_seq, DFT_N2, n1, F_WIDTH), BF16)
    tr, ti = pl.pallas_call(
        functools.partial(_fft1_kernel, tb=tb),
        grid=(n_seq, DFT_N2 // tb),
        in_specs=[
            pl.BlockSpec((None, n1, tb * F_WIDTH), lambda b, j: (b, 0, j)),
            pl.BlockSpec((n1, n1), lambda b, j: (0, 0)),
            pl.BlockSpec((n1, n1), lambda b, j: (0, 0)),
            pl.BlockSpec((tb, n1, LANES), lambda b, j: (j, 0, 0)),
            pl.BlockSpec((tb, n1, LANES), lambda b, j: (j, 0, 0)),
        ],
        out_specs=[pl.BlockSpec((None, tb, n1, F_WIDTH), lambda b, j: (b, j, 0, 0))] * 2,
        out_shape=[t_shape] * 2,
        compiler_params=_cparams(("parallel", "parallel")),
        name="fft_stage1",
    )(x, c1, s1, twc, tws)

    tn = min(4096, n1 * F_WIDTH)
    tr = tr.reshape(n_seq, DFT_N2, n1 * F_WIDTH)
    ti = ti.reshape(n_seq, DFT_N2, n1 * F_WIDTH)
    scale = 1.0 / math.sqrt(seq_len * FG_DIM)
    const = lambda b, j: (0, 0)
    f = pl.pallas_call(
        functools.partial(_fft2_kernel, scale=scale),
        grid=(n_seq, n1 * F_WIDTH // tn),
        in_specs=[
            pl.BlockSpec((None, DFT_N2, tn), lambda b, j: (b, 0, j)),
            pl.BlockSpec((None, DFT_N2, tn), lambda b, j: (b, 0, j)),
            pl.BlockSpec((DFT_N2, DFT_N2), const),
            pl.BlockSpec((DFT_N2, DFT_N2), const),
            pl.BlockSpec((FG_DIM, FG_DIM), const),
            pl.BlockSpec((FG_DIM, FG_DIM), const),
            pl.BlockSpec((N_FG, FG_DIM, FG_DIM), lambda b, j: (0, 0, 0)),
            pl.BlockSpec((N_FG, 1, FG_DIM), lambda b, j: (0, 0, 0)),
        ],
        out_specs=pl.BlockSpec((None, DFT_N2, tn), lambda b, j: (b, 0, j)),
        out_shape=jax.ShapeDtypeStruct((n_seq, DFT_N2, n1 * F_WIDTH), BF16),
        compiler_params=_cparams(("parallel", "parallel")),
        name="fft_stage2",
    )(tr, ti, c2, s2, c2, s2, w_f_b, b_f.reshape(N_FG, 1, FG_DIM))
    return f.reshape(n_seq * seq_len, F_WIDTH)


def _attn_finalize(lam_ref, subg_ref, o_ref, acc0, acc1, l0, l1, lam_init):
    lam_p = lam_ref[...]
    lam = (jnp.exp(jnp.sum(lam_p[0:1] * lam_p[1:2], axis=1, keepdims=True))
           - jnp.exp(jnp.sum(lam_p[2:3] * lam_p[3:4], axis=1, keepdims=True)) + lam_init)
    o = acc0 / l0 - lam * (acc1 / l1)
    ms_o = jnp.mean(o * o, axis=-1, keepdims=True)
    o = o * lax.rsqrt(ms_o + EPS) * subg_ref[...]
    o_ref[...] = (o * (1.0 - lam_init)).astype(BF16)


def _attn_kernel(bounded_ref, lam_ref, subg_ref, q_ref, k_ref, v_ref, o_ref,
                 q2_ref, acc_ref, m_ref, l_ref, ls_ref, *, seq_len, tk, lam_init):
    tq = q_ref.shape[0]
    q = q_ref[...]
    lane = lax.broadcasted_iota(jnp.int32, q.shape, 1)
    zero = jnp.zeros_like(q)
    q2_ref[0:tq, :] = jnp.where(lane < HEAD_DIM, q, zero)
    q2_ref[tq:2 * tq, :] = jnp.where(lane >= HEAD_DIM, q, zero)
    acc_ref[...] = jnp.zeros_like(acc_ref)
    nt = (((1,), (1,)), ((), ()))

    def scores(j):
        start = pl.multiple_of(j * tk, tk)
        s = lax.dot_general(q2_ref[...], k_ref[pl.ds(start, tk), :], nt,
                            preferred_element_type=F32)
        return s, v_ref[pl.ds(start, tk), :]

    def finalize(l):
        acc = acc_ref[...]
        _attn_finalize(lam_ref, subg_ref, o_ref, acc[0:tq], acc[tq:2 * tq],
                       l[0:tq], l[tq:2 * tq], lam_init)

    @pl.when(bounded_ref[0] == 1)
    def _():
        ls_ref[...] = jnp.zeros_like(ls_ref)

        def body(j, carry):
            s, vt = scores(j)
            p = jnp.exp(s)
            part = p[:, 0:LANES]
            for t in range(1, tk // LANES):
                part = part + p[:, t * LANES:(t + 1) * LANES]
            ls_ref[...] += part
            acc_ref[...] += jnp.dot(p.astype(BF16), vt, preferred_element_type=F32)
            return carry

        lax.fori_loop(0, seq_len // tk, body, 0, unroll=ATTN_UNROLL)
        finalize(jnp.sum(ls_ref[...], axis=1, keepdims=True))

    @pl.when(bounded_ref[0] == 0)
    def _():
        m_ref[...] = jnp.full_like(m_ref, -jnp.inf)
        l_ref[...] = jnp.zeros_like(l_ref)

        def body(j, carry):
            s, vt = scores(j)
            m_prev = m_ref[...]
            m_new = jnp.maximum(m_prev, jnp.max(s, axis=1, keepdims=True))
            alpha = jnp.exp(m_prev - m_new)
            p = jnp.exp(s - m_new)
            l_ref[...] = alpha * l_ref[...] + jnp.sum(p, axis=1, keepdims=True)
            acc_ref[...] = alpha * acc_ref[...] + jnp.dot(p.astype(BF16), vt,
                                                          preferred_element_type=F32)
            m_ref[...] = m_new
            return carry

        lax.fori_loop(0, seq_len // tk, body, 0)
        finalize(l_ref[...])


def _attention(q, k, v, bounded, lam_p, sub_g, *, row0, n_seq, seq_len, lam_init, tq, tk):
    nq = seq_len // tq
    q0 = row0 // tq
    kv0 = row0 // seq_len
    kernel = functools.partial(_attn_kernel, seq_len=seq_len, tk=tk, lam_init=lam_init)
    grid_spec = pltpu.PrefetchScalarGridSpec(
        num_scalar_prefetch=1,
        grid=(n_seq, N_HEADS, nq),
        in_specs=[
            pl.BlockSpec((4, HEAD_DIM), lambda b, h, i, f: (0, 0)),
            pl.BlockSpec((1, V_DIM), lambda b, h, i, f: (0, 0)),
            pl.BlockSpec((tq, LANES), lambda b, h, i, f: (q0 + b * nq + i, h)),
            pl.BlockSpec((seq_len, LANES), lambda b, h, i, f: (kv0 + b, h)),
            pl.BlockSpec((seq_len, LANES), lambda b, h, i, f: (kv0 + b, h)),
        ],
        out_specs=pl.BlockSpec((tq, V_DIM), lambda b, h, i, f: (b * nq + i, h)),
        scratch_shapes=[pltpu.VMEM((2 * tq, LANES), BF16), pltpu.VMEM((2 * tq, V_DIM), F32),
                        pltpu.VMEM((2 * tq, 1), F32), pltpu.VMEM((2 * tq, 1), F32),
                        pltpu.VMEM((2 * tq, LANES), F32)],
    )
    return pl.pallas_call(
        kernel,
        grid_spec=grid_spec,
        out_shape=jax.ShapeDtypeStruct((n_seq * seq_len, N_HEADS * V_DIM), BF16),
        compiler_params=_cparams(("parallel", "parallel", "arbitrary")),
        name="diff_attn",
    )(bounded, lam_p, sub_g, q, k, v)


def _outproj_kernel(f_ref, a_ref, x_ref, wo_ref, g2_ref, wr_ref, br_ref,
                    x1_ref, h2_ref, idx_ref, gexp_ref):
    acc = (jnp.dot(f_ref[...], wo_ref[0:F_WIDTH, :], preferred_element_type=F32)
           + jnp.dot(a_ref[...], wo_ref[F_WIDTH:D_MODEL, :], preferred_element_type=F32))
    x1 = x_ref[...] + acc
    x1_ref[...] = x1
    ms = jnp.mean(x1 * x1, axis=-1, keepdims=True)
    h2 = (x1 * lax.rsqrt(ms + EPS) * g2_ref[...]).astype(BF16)
    lo = lax.bitcast_convert_type(h2[:, 0:HALF_D].astype(F32), jnp.uint32)
    hi = lax.bitcast_convert_type(h2[:, HALF_D:D_MODEL].astype(F32), jnp.uint32)
    h2_ref[...] = lax.shift_right_logical(lo, jnp.uint32(16)) | (hi & jnp.uint32(0xFFFF0000))
    logits = jnp.dot(h2, wr_ref[...], preferred_element_type=F32) + br_ref[...]
    tm = logits.shape[0]
    lane = lax.broadcasted_iota(jnp.int32, (tm, N_EXPERTS), 1)
    cur = logits
    vals, idxs = [], []
    for _ in range(TOP_K):
        m = jnp.max(cur, axis=1, keepdims=True)
        am = jnp.min(jnp.where(cur == m, lane, N_EXPERTS), axis=1, keepdims=True)
        vals.append(m)
        idxs.append(am)
        cur = jnp.where(lane == am, -jnp.inf, cur)
    idx_ref[...] = jnp.concatenate(idxs, axis=1)
    es = [jnp.exp(v - vals[0]) for v in vals]
    den = es[0] + es[1] + es[2] + es[3]
    gexp_ref[...] = jnp.concatenate([jnp.broadcast_to(e / den, (tm, LANES)) for e in es], axis=1)


def _outproj(f, a, x, w_out_b, g2, w_r_b, b_r, *, tm):
    T = x.shape[0]
    const = lambda i: (0, 0)
    return pl.pallas_call(
        _outproj_kernel,
        grid=(T // tm,),
        in_specs=[
            pl.BlockSpec((tm, F_WIDTH), lambda i: (i, 0)),
            pl.BlockSpec((tm, F_WIDTH), lambda i: (i, 0)),
            pl.BlockSpec((tm, D_MODEL), lambda i: (i, 0)),
            pl.BlockSpec((D_MODEL, D_MODEL), const),
            pl.BlockSpec((1, D_MODEL), const),
            pl.BlockSpec((D_MODEL, N_EXPERTS), const),
            pl.BlockSpec((1, N_EXPERTS), const),
        ],
        out_specs=[
            pl.BlockSpec((tm, D_MODEL), lambda i: (i, 0)),
            pl.BlockSpec((tm, HALF_D), lambda i: (i, 0)),
            pl.BlockSpec((tm, TOP_K), lambda i: (i, 0)),
            pl.BlockSpec((tm, TOP_K * LANES), lambda i: (i, 0)),
        ],
        out_shape=[
            jax.ShapeDtypeStruct((T, D_MODEL), F32),
            jax.ShapeDtypeStruct((T, HALF_D), jnp.uint32),
            jax.ShapeDtypeStruct((T, TOP_K), jnp.int32),
            jax.ShapeDtypeStruct((T, TOP_K * LANES), F32),
        ],
        compiler_params=_cparams(("parallel",)),
        name="outproj_router",
    )(f, a, x, w_out_b, g2, w_r_b, b_r)


def _route(top_idx, n_blk):
    flat_e = top_idx.reshape(-1)
    onehot = (flat_e[:, None] == jnp.arange(N_EXPERTS, dtype=jnp.int32)[None, :]).astype(jnp.int32)
    csum = jnp.cumsum(onehot, axis=0)
    counts = csum[-1]
    rank = jnp.sum(csum * onehot, axis=1) - 1
    nblk_e = (counts + MOE_TM - 1) // MOE_TM
    blk_end = jnp.cumsum(nblk_e)
    blk_start = blk_end - nblk_e
    dest = jnp.sum(onehot * (blk_start * MOE_TM)[None, :], axis=1) + rank
    b = jnp.arange(n_blk, dtype=jnp.int32)
    n_used = blk_end[-1]
    blk_src = jnp.minimum(b, n_used - 1)
    blk_e = jnp.minimum(jnp.sum((blk_end[None, :] <= blk_src[:, None]).astype(jnp.int32), axis=1),
                        N_EXPERTS - 1)
    blk_used = (b < n_used).astype(jnp.int32)
    return dest.astype(jnp.int32), blk_e.astype(jnp.int32), blk_src.astype(jnp.int32), blk_used


def _w1_split_kernel(w_ref, perm_ref, g_ref, l_ref):
    perm = perm_ref[...]
    half = perm.shape[0] // 2
    for c in range(w_ref.shape[2] // perm.shape[0]):
        w = w_ref[0, :, c * 2 * half:(c + 1) * 2 * half].astype(BF16)
        r = jnp.dot(w, perm, preferred_element_type=F32)
        g_ref[0, :, c * half:(c + 1) * half] = r[:, :half].astype(BF16)
        l_ref[0, :, c * half:(c + 1) * half] = r[:, half:].astype(BF16)


def _w1_split(w1):
    n_e, d, f2 = w1.shape
    chunk = 2 * LANES
    tcol = 512
    i = jnp.arange(chunk, dtype=jnp.int32)
    src = jnp.where(i < LANES, 2 * i, 2 * (i - LANES) + 1)
    perm = (i[:, None] == src[None, :]).astype(BF16)
    out = jax.ShapeDtypeStruct((n_e, d, f2 // 2), BF16)
    return pl.pallas_call(
        _w1_split_kernel,
        grid=(n_e, f2 // tcol),
        in_specs=[pl.BlockSpec((1, d, tcol), lambda e, j: (e, 0, j)),
                  pl.BlockSpec((chunk, chunk), lambda e, j: (0, 0))],
        out_specs=[pl.BlockSpec((1, d, tcol // 2), lambda e, j: (e, 0, j))] * 2,
        out_shape=[out, out],
        compiler_params=_cparams(("parallel", "parallel")),
        name="w1_split",
    )(w1, perm)


def _scatter_kernel(dest_ref, h_ref, xs_in_ref, xs_ref, idx_smem, idx_sem, sem, *, te):
    del xs_in_ref
    cp = pltpu.make_async_copy(dest_ref.at[0, 0], idx_smem, idx_sem)
    cp.start()
    cp.wait()

    def row_copy(t, d):
        return pltpu.make_async_copy(h_ref.at[pl.ds(t, 1), :], xs_ref.at[pl.ds(d, 1), :], sem)

    def issue(t, carry):
        for k in range(TOP_K):
            row_copy(t, idx_smem[TOP_K * t + k]).start()
        return carry

    lax.fori_loop(0, te, issue, 0)

    def drain(t, carry):
        for k in range(TOP_K):
            row_copy(0, 0).wait()
        return carry

    lax.fori_loop(0, te, drain, 0)


def _moe_scatter(h2p, dest, n_rows, *, te):
    T = h2p.shape[0]
    dest3 = dest.reshape(T // te, 1, te * TOP_K)
    xs0 = jnp.zeros((n_rows, HALF_D), jnp.uint32)
    return pl.pallas_call(
        functools.partial(_scatter_kernel, te=te),
        grid=(T // te,),
        in_specs=[
            pl.BlockSpec((1, 1, te * TOP_K), lambda i: (i, 0, 0)),
            pl.BlockSpec((te, HALF_D), lambda i: (i, 0)),
            pl.BlockSpec(memory_space=pl.ANY),
        ],
        out_specs=pl.BlockSpec(memory_space=pl.ANY),
        out_shape=jax.ShapeDtypeStruct((n_rows, HALF_D), jnp.uint32),
        scratch_shapes=[pltpu.SMEM((te * TOP_K,), jnp.int32),
                        pltpu.SemaphoreType.DMA(()), pltpu.SemaphoreType.DMA(())],
        input_output_aliases={2: 0},
        compiler_params=_cparams(("arbitrary",)),
        name="moe_scatter",
    )(dest3, h2p, xs0)


def _moe_ffn_kernel(be_ref, bsrc_ref, bused_ref, x_ref, w1g_ref, w1l_ref, b1g_ref, b1l_ref,
                    w2_ref, b2_ref, o_ref, xb_ref):
    b = pl.program_id(0)
    f = pl.program_id(1)

    @pl.when(jnp.logical_and(bused_ref[b] > 0, f == 0))
    def _():
        w = x_ref[...]
        lo = lax.bitcast_convert_type(lax.shift_left(w, jnp.uint32(16)), F32)
        hi = lax.bitcast_convert_type(w & jnp.uint32(0xFFFF0000), F32)
        xb_ref[:, 0:HALF_D] = lo.astype(BF16)
        xb_ref[:, HALF_D:D_MODEL] = hi.astype(BF16)

    @pl.when(bused_ref[b] > 0)
    def _():
        x = xb_ref[...]
        hg = jnp.dot(x, w1g_ref[0], preferred_element_type=F32) + b1g_ref[0]
        hl = jnp.dot(x, w1l_ref[0], preferred_element_type=F32) + b1l_ref[0]
        hg = jnp.minimum(hg, SWIGLU_LIMIT)
        hl = jnp.clip(hl, -SWIGLU_LIMIT, SWIGLU_LIMIT)
        act = hg * jax.nn.sigmoid(SWIGLU_ALPHA * hg) * (hl + 1.0)
        part = jnp.dot(act.astype(BF16), w2_ref[0], preferred_element_type=F32)

        @pl.when(f == 0)
        def _():
            o_ref[...] = part + b2_ref[0]

        @pl.when(f > 0)
        def _():
            o_ref[...] += part

    @pl.when(jnp.logical_and(bused_ref[b] == 0, f == 0))
    def _():
        o_ref[...] = jnp.zeros_like(o_ref)


def _moe_ffn(xs, blk_e, blk_src, blk_used, w1g, w1l, b1g, b1l, w2b, b2):
    n_rows = xs.shape[0]
    n_blk = n_rows // MOE_TM
    n_f = D_FF // MOE_TF

    def f_idx(f, used):
        return jnp.where(used > 0, f, n_f - 1)

    grid_spec = pltpu.PrefetchScalarGridSpec(
        num_scalar_prefetch=3,
        grid=(n_blk, n_f),
        in_specs=[
            pl.BlockSpec((MOE_TM, HALF_D), lambda b, f, be, bs, bu: (bs[b], 0)),
            pl.BlockSpec((1, D_MODEL, MOE_TF), lambda b, f, be, bs, bu: (be[b], 0, f_idx(f, bu[b]))),
            pl.BlockSpec((1, D_MODEL, MOE_TF), lambda b, f, be, bs, bu: (be[b], 0, f_idx(f, bu[b]))),
            pl.BlockSpec((1, 1, MOE_TF), lambda b, f, be, bs, bu: (be[b], 0, f_idx(f, bu[b]))),
            pl.BlockSpec((1, 1, MOE_TF), lambda b, f, be, bs, bu: (be[b], 0, f_idx(f, bu[b]))),
            pl.BlockSpec((1, MOE_TF, D_MODEL), lambda b, f, be, bs, bu: (be[b], f_idx(f, bu[b]), 0)),
            pl.BlockSpec((1, 1, D_MODEL), lambda b, f, be, bs, bu: (be[b], 0, 0)),
        ],
        out_specs=pl.BlockSpec((MOE_TM, D_MODEL), lambda b, f, be, bs, bu: (b, 0)),
        scratch_shapes=[pltpu.VMEM((MOE_TM, D_MODEL), BF16)],
    )
    return pl.pallas_call(
        _moe_ffn_kernel,
        grid_spec=grid_spec,
        out_shape=jax.ShapeDtypeStruct((n_rows, D_MODEL), F32),
        compiler_params=_cparams(("arbitrary", "arbitrary")),
        name="moe_ffn",
    )(blk_e, blk_src, blk_used, xs, w1g, w1l, b1g, b1l, w2b, b2)


def _combine_kernel(dest_ref, x1_ref, gexp_ref, ys_ref, o_ref, idx_smem, gbuf, idx_sem, sem, *, tc):
    cp = pltpu.make_async_copy(dest_ref.at[0, 0], idx_smem, idx_sem)
    cp.start()
    cp.wait()

    def row_copy(d, r):
        return pltpu.make_async_copy(ys_ref.at[pl.ds(d, 1), :], gbuf.at[pl.ds(r, 1), :], sem)

    def issue(t, carry):
        for k in range(TOP_K):
            row_copy(idx_smem[TOP_K * t + k], k * tc + t).start()
        return carry

    lax.fori_loop(0, tc, issue, 0)

    def drain(t, carry):
        for k in range(TOP_K):
            row_copy(0, 0).wait()
        return carry

    lax.fori_loop(0, tc, drain, 0)

    acc = x1_ref[...]
    for k in range(TOP_K):
        gate = jnp.concatenate([gexp_ref[:, k * LANES:(k + 1) * LANES]] * (D_MODEL // LANES), axis=1)
        acc = acc + gate * gbuf[pl.ds(k * tc, tc), :]
    o_ref[...] = acc


def _moe_combine(x1, gexp, dest, ys, *, tc):
    T = x1.shape[0]
    return pl.pallas_call(
        functools.partial(_combine_kernel, tc=tc),
        grid=(T // tc,),
        in_specs=[
            pl.BlockSpec((1, 1, tc * TOP_K), lambda i: (i, 0, 0)),
            pl.BlockSpec((tc, D_MODEL), lambda i: (i, 0)),
            pl.BlockSpec((tc, TOP_K * LANES), lambda i: (i, 0)),
            pl.BlockSpec(memory_space=pl.ANY),
        ],
        out_specs=pl.BlockSpec((tc, D_MODEL), lambda i: (i, 0)),
        out_shape=jax.ShapeDtypeStruct((T, D_MODEL), F32),
        scratch_shapes=[pltpu.SMEM((tc * TOP_K,), jnp.int32),
                        pltpu.VMEM((tc * TOP_K, D_MODEL), F32),
                        pltpu.SemaphoreType.DMA(()), pltpu.SemaphoreType.DMA(())],
        compiler_params=_cparams(("arbitrary",)),
        name="moe_combine",
    )(dest.reshape(T // tc, 1, tc * TOP_K), x1, gexp, ys)


def _rope_tables(n_pos):
    inv = 1.0 / (ROPE_THETA ** (jnp.arange(0, HEAD_DIM, 2, dtype=F32) / HEAD_DIM))
    ang = jnp.arange(n_pos, dtype=F32)[:, None] * inv[None, :]
    ang = jnp.concatenate([ang, ang, ang, ang], axis=-1)
    return jnp.cos(ang), jnp.sin(ang)


def _layer(x, l, s_p, n_smp, s_s, norm1_g, w_in, q_norm_g, k_norm_g, lambda_q1, lambda_k1,
           lambda_q2, lambda_k2, sub_norm_g, w_fourier, b_fourier, w_out, norm2_g,
           w_router, b_router, w1, b1, w2, b2):
    T = x.shape[0]
    lam_init = 0.8 - 0.6 * math.exp(-0.3 * l)
    tm = min(512, s_s)

    cos, sin = _rope_tables(max(s_p, s_s))
    u, q, k, v = _inproj(
        x, norm1_g.reshape(1, D_MODEL), w_in.astype(BF16), cos, sin,
        jnp.tile(q_norm_g, 2).reshape(1, LANES), jnp.tile(k_norm_g, 2).reshape(1, LANES),
        tm=tm, n_p=s_p // tm, n_s=s_s // tm)

    w_f_b = w_fourier.astype(BF16)
    f = jnp.concatenate([
        _fourier(u[:s_p], w_f_b, b_fourier, n_seq=1, seq_len=s_p),
        _fourier(u[s_p:], w_f_b, b_fourier, n_seq=n_smp, seq_len=s_s)], axis=0)

    lam_p = jnp.stack([lambda_q1, lambda_k1, lambda_q2, lambda_k2]).astype(F32)
    sub_g = sub_norm_g.reshape(1, V_DIM)
    score_bound = (1.01 * math.sqrt(HEAD_DIM)) * jnp.max(jnp.abs(q_norm_g)) * jnp.max(jnp.abs(k_norm_g))
    bounded = (score_bound <= SCORE_BOUND_NO_MAX).astype(jnp.int32).reshape(1)
    a = jnp.concatenate([
        _attention(q, k, v, bounded, lam_p, sub_g, row0=0, n_seq=1, seq_len=s_p,
                   lam_init=lam_init, tq=min(ATTN_TQ, s_p), tk=min(ATTN_TK, s_p)),
        _attention(q, k, v, bounded, lam_p, sub_g, row0=s_p, n_seq=n_smp, seq_len=s_s,
                   lam_init=lam_init, tq=min(ATTN_TQ, s_s), tk=min(ATTN_TK, s_s))], axis=0)

    x1, h2, top_idx, gexp = _outproj(
        f, a, x, w_out.astype(BF16), norm2_g.reshape(1, D_MODEL), w_router.astype(BF16),
        b_router.reshape(1, N_EXPERTS), tm=tm)

    n_blk = T * TOP_K // MOE_TM + N_EXPERTS
    dest, blk_e, blk_src, blk_used = _route(top_idx, n_blk)
    xs = _moe_scatter(h2, dest, n_blk * MOE_TM, te=min(256, s_s))
    w1g, w1l = _w1_split(w1)
    ys = _moe_ffn(
        xs, blk_e, blk_src, blk_used, w1g, w1l,
        b1[:, 0::2].reshape(N_EXPERTS, 1, D_FF), b1[:, 1::2].reshape(N_EXPERTS, 1, D_FF),
        w2.astype(BF16), b2.reshape(N_EXPERTS, 1, D_MODEL))
    return _moe_combine(x1, gexp, dest, ys, tc=min(128, s_s))


def kernel(x_prompt, x_sample, norm1_g, w_in, q_norm_g, k_norm_g, lambda_q1, lambda_k1, lambda_q2,
           lambda_k2, sub_norm_g, w_fourier, b_fourier, w_out, norm2_g, w_router, b_router,
           w1, b1, w2, b2):
    b_p, s_p, d = x_prompt.shape
    n_smp, s_s, _ = x_sample.shape
    assert b_p == 1 and d == D_MODEL
    x = jnp.concatenate([x_prompt.reshape(s_p, d), x_sample.reshape(n_smp * s_s, d)], axis=0)
    for l in range(norm1_g.shape[0]):
        x = _layer(x, l, s_p, n_smp, s_s, norm1_g[l], w_in[l], q_norm_g[l], k_norm_g[l],
                   lambda_q1[l], lambda_k1[l], lambda_q2[l], lambda_k2[l], sub_norm_g[l],
                   w_fourier[l], b_fourier[l], w_out[l], norm2_g[l], w_router[l], b_router[l],
                   w1[l], b1[l], w2[l], b2[l])
    return (x[:s_p].reshape(1, s_p, d), x[s_p:].reshape(n_smp, s_s, d))
```

```python
import functools
import math

import jax
import jax.numpy as jnp
import numpy as np
from jax import lax
from jax.experimental import pallas as pl
from jax.experimental.pallas import tpu as pltpu

F32 = jnp.float32
BF16 = jnp.bfloat16

D_MODEL = 2048
F_WIDTH = 1024
N_FG = 8
FG_DIM = 128
N_HEADS = 8
HEAD_DIM = 64
V_DIM = 128
QK_WIDTH = 1024
N_EXPERTS = 32
TOP_K = 4
D_FF = 2048
ROPE_THETA = 10000.0
SWIGLU_ALPHA = 1.702
SWIGLU_LIMIT = 7.0
EPS = 1e-5

LANES = 128
HALF_D = D_MODEL // 2
DFT_N2 = 128
VMEM_LIMIT = 48 * 1024 * 1024

MOE_TM = 512
MOE_TF = 1024
ATTN_TQ = 512
ATTN_TK = 512
ATTN_UNROLL = 8
SCORE_BOUND_NO_MAX = 40.0


def _cparams(sem):
    return pltpu.CompilerParams(dimension_semantics=sem, vmem_limit_bytes=VMEM_LIMIT)


def _norm_rope_store(acc, g_ref, cos_ref, sin_ref, o_ref, scale):
    tm = acc.shape[0]
    w = 2 * LANES
    r = lax.broadcasted_iota(jnp.int32, (w, w), 0) // HEAD_DIM
    c = lax.broadcasted_iota(jnp.int32, (w, w), 1) // HEAD_DIM
    group_mean = jnp.where(r == c, 1.0 / HEAD_DIM, 0.0).astype(BF16)
    lane = lax.broadcasted_iota(jnp.int32, (tm, w), 1)
    first_half = (lane % HEAD_DIM) < (HEAD_DIM // 2)
    cos = jnp.concatenate([cos_ref[...]] * 2, axis=1)
    sin = jnp.concatenate([sin_ref[...]] * 2, axis=1)
    g = jnp.concatenate([g_ref[...]] * 2, axis=1)
    for h in range(acc.shape[1] // w):
        a = acc[:, h * w:(h + 1) * w]
        sq = a * a
        hi = sq.astype(BF16)
        lo = (sq - hi.astype(F32)).astype(BF16)
        ms = (jnp.dot(hi, group_mean, preferred_element_type=F32)
              + jnp.dot(lo, group_mean, preferred_element_type=F32))
        y = a * lax.rsqrt(ms + EPS) * g
        rot = jnp.where(first_half, -pltpu.roll(y, w - HEAD_DIM // 2, 1),
                        pltpu.roll(y, HEAD_DIM // 2, 1))
        o_ref[:, h * w:(h + 1) * w] = ((y * cos + rot * sin) * scale).astype(BF16)


def _pick(n_first, first_ref, second_ref):
    return jnp.where(pl.program_id(0) < n_first, first_ref[...], second_ref[...])


def _pair_specs(tm, width, n_first, n_grid_axes):
    if n_grid_axes == 1:
        return [pl.BlockSpec((tm, width), lambda i: (jnp.minimum(i, n_first - 1), 0)),
                pl.BlockSpec((tm, width), lambda i: (jnp.maximum(i - n_first, 0), 0))]
    return [pl.BlockSpec((tm, width), lambda i, j: (jnp.minimum(i, n_first - 1), 0)),
            pl.BlockSpec((tm, width), lambda i, j: (jnp.maximum(i - n_first, 0), 0))]


def _inproj_kernel(xp_ref, xs_ref, g1_ref, w_ref, cos_ref, sin_ref, gq_ref, gk_ref,
                   u_ref, q_ref, k_ref, v_ref, h_ref, *, n_p):
    j = pl.program_id(1)

    @pl.when(j == 0)
    def _():
        x = _pick(n_p, xp_ref, xs_ref)
        ms = jnp.mean(x * x, axis=-1, keepdims=True)
        h_ref[...] = (x * lax.rsqrt(ms + EPS) * g1_ref[...]).astype(BF16)

    acc = jnp.dot(h_ref[...], w_ref[...], preferred_element_type=F32)

    @pl.when(j == 0)
    def _():
        u_ref[...] = acc

    @pl.when(j == 1)
    def _():
        _norm_rope_store(acc, gq_ref, cos_ref, sin_ref, q_ref, 1.0 / math.sqrt(HEAD_DIM))

    @pl.when(j == 2)
    def _():
        _norm_rope_store(acc, gk_ref, cos_ref, sin_ref, k_ref, 1.0)

    @pl.when(j == 3)
    def _():
        v_ref[...] = acc.astype(BF16)


def _inproj(xp, xs, g1, w_in_b, cos, sin, gq, gk, *, tm, n_p, n_s):
    T = xp.shape[0] + xs.shape[0]

    def pos_blk(i, j):
        return (jnp.where(i < n_p, i, (i - n_p) % n_s), 0)

    out = jax.ShapeDtypeStruct((T, 1024), BF16)
    out_u = jax.ShapeDtypeStruct((T, 1024), F32)
    return pl.pallas_call(
        functools.partial(_inproj_kernel, n_p=n_p),
        grid=(T // tm, 4),
        in_specs=_pair_specs(tm, D_MODEL, n_p, 2) + [
            pl.BlockSpec((1, D_MODEL), lambda i, j: (0, 0)),
            pl.BlockSpec((D_MODEL, 1024), lambda i, j: (0, j)),
            pl.BlockSpec((tm, LANES), pos_blk),
            pl.BlockSpec((tm, LANES), pos_blk),
            pl.BlockSpec((1, LANES), lambda i, j: (0, 0)),
            pl.BlockSpec((1, LANES), lambda i, j: (0, 0)),
        ],
        out_specs=[pl.BlockSpec((tm, 1024), lambda i, j: (i, 0))] * 4,
        out_shape=[out_u, out, out, out],
        scratch_shapes=[pltpu.VMEM((tm, D_MODEL), BF16)],
        compiler_params=_cparams(("parallel", "arbitrary")),
        name="inproj",
    )(xp, xs, g1, w_in_b, cos, sin, gq, gk)


def _fft1_kernel(x_ref, c1_ref, s1_ref, twc_ref, tws_ref, tr_ref, ti_ref):
    c1 = c1_ref[...]
    s1 = s1_ref[...]
    for t in range(x_ref.shape[1]):
        x = x_ref[:, t, :].astype(BF16)
        a = jnp.dot(c1, x, preferred_element_type=F32)
        b = -jnp.dot(s1, x, preferred_element_type=F32)
        c = jnp.concatenate([twc_ref[t]] * N_FG, axis=1)
        s = jnp.concatenate([tws_ref[t]] * N_FG, axis=1)
        tr_ref[t] = a * c + b * s
        ti_ref[t] = b * c - a * s


def _fft2_kernel(tr_ref, ti_ref, c2_ref, s2_ref, cc_ref, sc_ref, wf_ref, bf_ref, o_ref, *, scale):
    c2 = c2_ref[...]
    s2 = s2_ref[...]
    cc = cc_ref[...]
    sc = sc_ref[...]
    for i in range(tr_ref.shape[1]):
        tr = tr_ref[:, i, :].astype(BF16)
        ti = ti_ref[:, i, :].astype(BF16)
        ur = (jnp.dot(c2, tr, preferred_element_type=F32) + jnp.dot(s2, ti, preferred_element_type=F32))
        ui = (jnp.dot(c2, ti, preferred_element_type=F32) - jnp.dot(s2, tr, preferred_element_type=F32))
        outs = []
        for g in range(N_FG):
            sl = slice(g * FG_DIM, (g + 1) * FG_DIM)
            y = (jnp.dot(ur[:, sl].astype(BF16), cc, preferred_element_type=F32)
                 + jnp.dot(ui[:, sl].astype(BF16), sc, preferred_element_type=F32))
            y = (y * scale).astype(BF16)
            outs.append(jnp.dot(y, wf_ref[g], preferred_element_type=F32) + bf_ref[g])
        o_ref[:, i, :] = jnp.concatenate(outs, axis=1)


@functools.lru_cache(maxsize=None)
def _dft_mats_np(n):
    j = np.arange(n)
    ang = (2.0 * np.pi / n) * ((j[:, None] * j[None, :]) % n)
    return np.cos(ang).astype(np.float32), np.sin(ang).astype(np.float32)


def _dft_mats(n):
    c, s = _dft_mats_np(n)
    return jnp.asarray(c).astype(BF16), jnp.asarray(s).astype(BF16)


@functools.lru_cache(maxsize=None)
def _twiddle_np(n1, seq_len):
    ang = (2.0 * np.pi / seq_len) * (np.arange(DFT_N2)[:, None] * np.arange(n1)[None, :])
    return np.cos(ang).astype(np.float32), np.sin(ang).astype(np.float32)


def _fourier(u3, w_f_b, b_f, *, row0, n_seq, seq_len):
    n1 = seq_len // DFT_N2
    blk0 = row0 // seq_len
    c1, s1 = _dft_mats(n1)
    c2, s2 = _dft_mats(DFT_N2)
    tw_cos, tw_sin = _twiddle_np(n1, seq_len)
    twc = jnp.broadcast_to(jnp.asarray(tw_cos)[:, :, None], (DFT_N2, n1, LANES))
    tws = jnp.broadcast_to(jnp.asarray(tw_sin)[:, :, None], (DFT_N2, n1, LANES))

    tb = 8
    t_shape = jax.ShapeDtypeStruct((n_seq, DFT_N2, n1, F_WIDTH), F32)
    tr, ti = pl.pallas_call(
        _fft1_kernel,
        grid=(n_seq, DFT_N2 // tb),
        in_specs=[
            pl.BlockSpec((n1, tb, F_WIDTH), lambda b, j: (blk0 + b, j, 0)),
            pl.BlockSpec((n1, n1), lambda b, j: (0, 0)),
            pl.BlockSpec((n1, n1), lambda b, j: (0, 0)),
            pl.BlockSpec((tb, n1, LANES), lambda b, j: (j, 0, 0)),
            pl.BlockSpec((tb, n1, LANES), lambda b, j: (j, 0, 0)),
        ],
        out_specs=[pl.BlockSpec((None, tb, n1, F_WIDTH), lambda b, j: (b, j, 0, 0))] * 2,
        out_shape=[t_shape] * 2,
        compiler_params=_cparams(("parallel", "parallel")),
        name="fft_stage1",
    )(u3, c1, s1, twc, tws)

    tk1 = min(8, n1)
    scale = 1.0 / math.sqrt(seq_len * FG_DIM)
    const = lambda b, j: (0, 0)
    f = pl.pallas_call(
        functools.partial(_fft2_kernel, scale=scale),
        grid=(n_seq, n1 // tk1),
        in_specs=[
            pl.BlockSpec((None, DFT_N2, tk1, F_WIDTH), lambda b, j: (b, 0, j, 0)),
            pl.BlockSpec((None, DFT_N2, tk1, F_WIDTH), lambda b, j: (b, 0, j, 0)),
            pl.BlockSpec((DFT_N2, DFT_N2), const),
            pl.BlockSpec((DFT_N2, DFT_N2), const),
            pl.BlockSpec((FG_DIM, FG_DIM), const),
            pl.BlockSpec((FG_DIM, FG_DIM), const),
            pl.BlockSpec((N_FG, FG_DIM, FG_DIM), lambda b, j: (0, 0, 0)),
            pl.BlockSpec((N_FG, 1, FG_DIM), lambda b, j: (0, 0, 0)),
        ],
        out_specs=pl.BlockSpec((None, DFT_N2, tk1, F_WIDTH), lambda b, j: (b, 0, j, 0)),
        out_shape=jax.ShapeDtypeStruct((n_seq, DFT_N2, n1, F_WIDTH), F32),
        compiler_params=_cparams(("parallel", "parallel")),
        name="fft_stage2",
    )(tr, ti, c2, s2, c2, s2, w_f_b, b_f.reshape(N_FG, 1, FG_DIM))
    return f.reshape(n_seq * seq_len, F_WIDTH)


def _attn_finalize(lam_ref, subg_ref, o_ref, acc0, acc1, l0, l1, lam_init):
    lam_p = lam_ref[...]
    lam = (jnp.exp(jnp.sum(lam_p[0:1] * lam_p[1:2], axis=1, keepdims=True))
           - jnp.exp(jnp.sum(lam_p[2:3] * lam_p[3:4], axis=1, keepdims=True)) + lam_init)
    o = acc0 / l0 - lam * (acc1 / l1)
    ms_o = jnp.mean(o * o, axis=-1, keepdims=True)
    o = o * lax.rsqrt(ms_o + EPS) * subg_ref[...]
    o_ref[...] = (o * (1.0 - lam_init)).astype(BF16)


def _attn_kernel(bounded_ref, lam_ref, subg_ref, q_ref, k_ref, v_ref, o_ref,
                 q2_ref, acc_ref, m_ref, l_ref, ls_ref, *, seq_len, tk, lam_init):
    tq = q_ref.shape[0]
    q = q_ref[...]
    lane = lax.broadcasted_iota(jnp.int32, q.shape, 1)
    zero = jnp.zeros_like(q)
    q2_ref[0:tq, :] = jnp.where(lane < HEAD_DIM, q, zero)
    q2_ref[tq:2 * tq, :] = jnp.where(lane >= HEAD_DIM, q, zero)
    acc_ref[...] = jnp.zeros_like(acc_ref)
    nt = (((1,), (1,)), ((), ()))

    def scores(j):
        start = pl.multiple_of(j * tk, tk)
        s = lax.dot_general(q2_ref[...], k_ref[pl.ds(start, tk), :], nt,
                            preferred_element_type=F32)
        return s, v_ref[pl.ds(start, tk), :]

    def finalize(l):
        acc = acc_ref[...]
        _attn_finalize(lam_ref, subg_ref, o_ref, acc[0:tq], acc[tq:2 * tq],
                       l[0:tq], l[tq:2 * tq], lam_init)

    @pl.when(bounded_ref[0] == 1)
    def _():
        ls_ref[...] = jnp.zeros_like(ls_ref)

        def body(j, carry):
            s, vt = scores(j)
            p = jnp.exp(s)
            part = p[:, 0:LANES]
            for t in range(1, tk // LANES):
                part = part + p[:, t * LANES:(t + 1) * LANES]
            ls_ref[...] += part
            acc_ref[...] += jnp.dot(p.astype(BF16), vt, preferred_element_type=F32)
            return carry

        lax.fori_loop(0, seq_len // tk, body, 0, unroll=ATTN_UNROLL)
        finalize(jnp.sum(ls_ref[...], axis=1, keepdims=True))

    @pl.when(bounded_ref[0] == 0)
    def _():
        m_ref[...] = jnp.full_like(m_ref, -jnp.inf)
        l_ref[...] = jnp.zeros_like(l_ref)

        def body(j, carry):
            s, vt = scores(j)
            m_prev = m_ref[...]
            m_new = jnp.maximum(m_prev, jnp.max(s, axis=1, keepdims=True))
            alpha = jnp.exp(m_prev - m_new)
            p = jnp.exp(s - m_new)
            l_ref[...] = alpha * l_ref[...] + jnp.sum(p, axis=1, keepdims=True)
            acc_ref[...] = alpha * acc_ref[...] + jnp.dot(p.astype(BF16), vt,
                                                          preferred_element_type=F32)
            m_ref[...] = m_new
            return carry

        lax.fori_loop(0, seq_len // tk, body, 0)
        finalize(l_ref[...])


def _attention(q, k, v, bounded, lam_p, sub_g, *, row0, n_seq, seq_len, lam_init, tq, tk):
    nq = seq_len // tq
    q0 = row0 // tq
    kv0 = row0 // seq_len
    kernel = functools.partial(_attn_kernel, seq_len=seq_len, tk=tk, lam_init=lam_init)
    grid_spec = pltpu.PrefetchScalarGridSpec(
        num_scalar_prefetch=1,
        grid=(n_seq, N_HEADS, nq),
        in_specs=[
            pl.BlockSpec((4, HEAD_DIM), lambda b, h, i, f: (0, 0)),
            pl.BlockSpec((1, V_DIM), lambda b, h, i, f: (0, 0)),
            pl.BlockSpec((tq, LANES), lambda b, h, i, f: (q0 + b * nq + i, h)),
            pl.BlockSpec((seq_len, LANES), lambda b, h, i, f: (kv0 + b, h)),
            pl.BlockSpec((seq_len, LANES), lambda b, h, i, f: (kv0 + b, h)),
        ],
        out_specs=pl.BlockSpec((tq, V_DIM), lambda b, h, i, f: (b * nq + i, h)),
        scratch_shapes=[pltpu.VMEM((2 * tq, LANES), BF16), pltpu.VMEM((2 * tq, V_DIM), F32),
                        pltpu.VMEM((2 * tq, 1), F32), pltpu.VMEM((2 * tq, 1), F32),
                        pltpu.VMEM((2 * tq, LANES), F32)],
    )
    return pl.pallas_call(
        kernel,
        grid_spec=grid_spec,
        out_shape=jax.ShapeDtypeStruct((n_seq * seq_len, N_HEADS * V_DIM), BF16),
        compiler_params=_cparams(("parallel", "parallel", "arbitrary")),
        name="diff_attn",
    )(bounded, lam_p, sub_g, q, k, v)


def _outproj_kernel(fp_ref, fs_ref, ap_ref, as_ref, xp_ref, xs_ref, wo_ref, g2_ref, wr_ref, br_ref,
                    x1_ref, h2_ref, idx_ref, gexp_ref, *, n_p):
    acc = (jnp.dot(_pick(n_p, fp_ref, fs_ref).astype(BF16), wo_ref[0:F_WIDTH, :],
                   preferred_element_type=F32)
           + jnp.dot(_pick(n_p, ap_ref, as_ref), wo_ref[F_WIDTH:D_MODEL, :],
                     preferred_element_type=F32))
    x1 = _pick(n_p, xp_ref, xs_ref) + acc
    x1_ref[...] = x1
    ms = jnp.mean(x1 * x1, axis=-1, keepdims=True)
    h2 = (x1 * lax.rsqrt(ms + EPS) * g2_ref[...]).astype(BF16)
    lo = lax.bitcast_convert_type(h2[:, 0:HALF_D].astype(F32), jnp.uint32)
    hi = lax.bitcast_convert_type(h2[:, HALF_D:D_MODEL].astype(F32), jnp.uint32)
    h2_ref[...] = lax.shift_right_logical(lo, jnp.uint32(16)) | (hi & jnp.uint32(0xFFFF0000))
    logits = jnp.dot(h2, wr_ref[...], preferred_element_type=F32) + br_ref[...]
    tm = logits.shape[0]
    lane = lax.broadcasted_iota(jnp.int32, (tm, N_EXPERTS), 1)
    cur = logits
    vals, idxs = [], []
    for _ in range(TOP_K):
        m = jnp.max(cur, axis=1, keepdims=True)
        am = jnp.min(jnp.where(cur == m, lane, N_EXPERTS), axis=1, keepdims=True)
        vals.append(m)
        idxs.append(am)
        cur = jnp.where(lane == am, -jnp.inf, cur)
    idx_ref[...] = jnp.concatenate(idxs, axis=1)
    es = [jnp.exp(v - vals[0]) for v in vals]
    den = es[0] + es[1] + es[2] + es[3]
    gexp_ref[...] = jnp.concatenate([jnp.broadcast_to(e / den, (tm, LANES)) for e in es], axis=1)


def _outproj(f_pair, a_pair, x_pair, w_out_b, g2, w_r_b, b_r, *, tm):
    T = x_pair[0].shape[0] + x_pair[1].shape[0]
    n_p = x_pair[0].shape[0] // tm
    const = lambda i: (0, 0)
    return pl.pallas_call(
        functools.partial(_outproj_kernel, n_p=n_p),
        grid=(T // tm,),
        in_specs=_pair_specs(tm, F_WIDTH, n_p, 1) + _pair_specs(tm, F_WIDTH, n_p, 1)
        + _pair_specs(tm, D_MODEL, n_p, 1) + [
            pl.BlockSpec((D_MODEL, D_MODEL), const),
            pl.BlockSpec((1, D_MODEL), const),
            pl.BlockSpec((D_MODEL, N_EXPERTS), const),
            pl.BlockSpec((1, N_EXPERTS), const),
        ],
        out_specs=[
            pl.BlockSpec((tm, D_MODEL), lambda i: (i, 0)),
            pl.BlockSpec((tm, HALF_D), lambda i: (i, 0)),
            pl.BlockSpec((tm, TOP_K), lambda i: (i, 0)),
            pl.BlockSpec((tm, TOP_K * LANES), lambda i: (i, 0)),
        ],
        out_shape=[
            jax.ShapeDtypeStruct((T, D_MODEL), F32),
            jax.ShapeDtypeStruct((T, HALF_D), jnp.uint32),
            jax.ShapeDtypeStruct((T, TOP_K), jnp.int32),
            jax.ShapeDtypeStruct((T, TOP_K * LANES), F32),
        ],
        compiler_params=_cparams(("parallel",)),
        name="outproj_router",
    )(*f_pair, *a_pair, *x_pair, w_out_b, g2, w_r_b, b_r)


def _route(top_idx, n_blk):
    flat_e = top_idx.reshape(-1)
    onehot = (flat_e[:, None] == jnp.arange(N_EXPERTS, dtype=jnp.int32)[None, :]).astype(jnp.int32)
    csum = jnp.cumsum(onehot, axis=0)
    counts = csum[-1]
    rank = jnp.sum(csum * onehot, axis=1) - 1
    nblk_e = (counts + MOE_TM - 1) // MOE_TM
    blk_end = jnp.cumsum(nblk_e)
    blk_start = blk_end - nblk_e
    dest = jnp.sum(onehot * (blk_start * MOE_TM)[None, :], axis=1) + rank
    b = jnp.arange(n_blk, dtype=jnp.int32)
    n_used = blk_end[-1]
    blk_src = jnp.minimum(b, n_used - 1)
    blk_e = jnp.minimum(jnp.sum((blk_end[None, :] <= blk_src[:, None]).astype(jnp.int32), axis=1),
                        N_EXPERTS - 1)
    blk_used = (b < n_used).astype(jnp.int32)
    return dest.astype(jnp.int32), blk_e.astype(jnp.int32), blk_src.astype(jnp.int32), blk_used


def _w1_split_kernel(w_ref, perm_ref, g_ref, l_ref):
    perm = perm_ref[...]
    half = perm.shape[0] // 2
    for c in range(w_ref.shape[2] // perm.shape[0]):
        w = w_ref[0, :, c * 2 * half:(c + 1) * 2 * half].astype(BF16)
        r = jnp.dot(w, perm, preferred_element_type=F32)
        g_ref[0, :, c * half:(c + 1) * half] = r[:, :half].astype(BF16)
        l_ref[0, :, c * half:(c + 1) * half] = r[:, half:].astype(BF16)


def _w1_split(w1):
    n_e, d, f2 = w1.shape
    chunk = 2 * LANES
    tcol = 512
    i = jnp.arange(chunk, dtype=jnp.int32)
    src = jnp.where(i < LANES, 2 * i, 2 * (i - LANES) + 1)
    perm = (i[:, None] == src[None, :]).astype(BF16)
    out = jax.ShapeDtypeStruct((n_e, d, f2 // 2), BF16)
    return pl.pallas_call(
        _w1_split_kernel,
        grid=(n_e, f2 // tcol),
        in_specs=[pl.BlockSpec((1, d, tcol), lambda e, j: (e, 0, j)),
                  pl.BlockSpec((chunk, chunk), lambda e, j: (0, 0))],
        out_specs=[pl.BlockSpec((1, d, tcol // 2), lambda e, j: (e, 0, j))] * 2,
        out_shape=[out, out],
        compiler_params=_cparams(("parallel", "parallel")),
        name="w1_split",
    )(w1, perm)


def _scatter_kernel(dest_ref, h_ref, xs_in_ref, xs_ref, idx_smem, idx_sem, sem, *, te):
    del xs_in_ref
    cp = pltpu.make_async_copy(dest_ref.at[0, 0], idx_smem, idx_sem)
    cp.start()
    cp.wait()

    def row_copy(t, d):
        return pltpu.make_async_copy(h_ref.at[pl.ds(t, 1), :], xs_ref.at[pl.ds(d, 1), :], sem)

    def issue(t, carry):
        for k in range(TOP_K):
            row_copy(t, idx_smem[TOP_K * t + k]).start()
        return carry

    lax.fori_loop(0, te, issue, 0)
    for k in range(TOP_K):
        pltpu.make_async_copy(h_ref, xs_ref.at[pl.ds(0, te), :], sem).wait()


def _moe_scatter(h2p, dest, n_rows, *, te):
    T = h2p.shape[0]
    dest3 = dest.reshape(T // te, 1, te * TOP_K)
    xs0 = jnp.zeros((n_rows, HALF_D), jnp.uint32)
    return pl.pallas_call(
        functools.partial(_scatter_kernel, te=te),
        grid=(T // te,),
        in_specs=[
            pl.BlockSpec((1, 1, te * TOP_K), lambda i: (i, 0, 0)),
            pl.BlockSpec((te, HALF_D), lambda i: (i, 0)),
            pl.BlockSpec(memory_space=pl.ANY),
        ],
        out_specs=pl.BlockSpec(memory_space=pl.ANY),
        out_shape=jax.ShapeDtypeStruct((n_rows, HALF_D), jnp.uint32),
        scratch_shapes=[pltpu.SMEM((te * TOP_K,), jnp.int32),
                        pltpu.SemaphoreType.DMA(()), pltpu.SemaphoreType.DMA(())],
        input_output_aliases={2: 0},
        compiler_params=_cparams(("arbitrary",)),
        name="moe_scatter",
    )(dest3, h2p, xs0)


def _moe_ffn_kernel(be_ref, bsrc_ref, bused_ref, x_ref, w1g_ref, w1l_ref, b1g_ref, b1l_ref,
                    w2_ref, b2_ref, o_ref, xb_ref):
    b = pl.program_id(0)
    f = pl.program_id(1)

    @pl.when(jnp.logical_and(bused_ref[b] > 0, f == 0))
    def _():
        w = x_ref[...]
        lo = lax.bitcast_convert_type(lax.shift_left(w, jnp.uint32(16)), F32)
        hi = lax.bitcast_convert_type(w & jnp.uint32(0xFFFF0000), F32)
        xb_ref[:, 0:HALF_D] = lo.astype(BF16)
        xb_ref[:, HALF_D:D_MODEL] = hi.astype(BF16)

    @pl.when(bused_ref[b] > 0)
    def _():
        x = xb_ref[...]
        hg = jnp.dot(x, w1g_ref[0], preferred_element_type=F32) + b1g_ref[0]
        hl = jnp.dot(x, w1l_ref[0], preferred_element_type=F32) + b1l_ref[0]
        hg = jnp.minimum(hg, SWIGLU_LIMIT)
        hl = jnp.clip(hl, -SWIGLU_LIMIT, SWIGLU_LIMIT)
        act = hg * jax.nn.sigmoid(SWIGLU_ALPHA * hg) * (hl + 1.0)
        part = jnp.dot(act.astype(BF16), w2_ref[0], preferred_element_type=F32)

        @pl.when(f == 0)
        def _():
            o_ref[...] = part + b2_ref[0]

        @pl.when(f > 0)
        def _():
            o_ref[...] += part

    @pl.when(jnp.logical_and(bused_ref[b] == 0, f == 0))
    def _():
        o_ref[...] = jnp.zeros_like(o_ref)


def _moe_ffn(xs, blk_e, blk_src, blk_used, w1g, w1l, b1g, b1l, w2b, b2):
    n_rows = xs.shape[0]
    n_blk = n_rows // MOE_TM
    n_f = D_FF // MOE_TF

    def f_idx(f, used):
        return jnp.where(used > 0, f, n_f - 1)

    grid_spec = pltpu.PrefetchScalarGridSpec(
        num_scalar_prefetch=3,
        grid=(n_blk, n_f),
        in_specs=[
            pl.BlockSpec((MOE_TM, HALF_D), lambda b, f, be, bs, bu: (bs[b], 0)),
            pl.BlockSpec((1, D_MODEL, MOE_TF), lambda b, f, be, bs, bu: (be[b], 0, f_idx(f, bu[b]))),
            pl.BlockSpec((1, D_MODEL, MOE_TF), lambda b, f, be, bs, bu: (be[b], 0, f_idx(f, bu[b]))),
            pl.BlockSpec((1, 1, MOE_TF), lambda b, f, be, bs, bu: (be[b], 0, f_idx(f, bu[b]))),
            pl.BlockSpec((1, 1, MOE_TF), lambda b, f, be, bs, bu: (be[b], 0, f_idx(f, bu[b]))),
            pl.BlockSpec((1, MOE_TF, D_MODEL), lambda b, f, be, bs, bu: (be[b], f_idx(f, bu[b]), 0)),
            pl.BlockSpec((1, 1, D_MODEL), lambda b, f, be, bs, bu: (be[b], 0, 0)),
        ],
        out_specs=pl.BlockSpec((MOE_TM, D_MODEL), lambda b, f, be, bs, bu: (b, 0)),
        scratch_shapes=[pltpu.VMEM((MOE_TM, D_MODEL), BF16)],
    )
    return pl.pallas_call(
        _moe_ffn_kernel,
        grid_spec=grid_spec,
        out_shape=jax.ShapeDtypeStruct((n_rows, D_MODEL), F32),
        compiler_params=_cparams(("arbitrary", "arbitrary")),
        name="moe_ffn",
    )(blk_e, blk_src, blk_used, xs, w1g, w1l, b1g, b1l, w2b, b2)


def _combine_kernel(dest_ref, x1_ref, gexp_ref, ys_ref, o_ref, idx_smem, gbuf, idx_sem, sem, *, tc):
    cp = pltpu.make_async_copy(dest_ref.at[0, 0], idx_smem, idx_sem)
    cp.start()
    cp.wait()

    def row_copy(d, r):
        return pltpu.make_async_copy(ys_ref.at[pl.ds(d, 1), :], gbuf.at[pl.ds(r, 1), :], sem)

    def issue(t, carry):
        for k in range(TOP_K):
            row_copy(idx_smem[TOP_K * t + k], k * tc + t).start()
        return carry

    lax.fori_loop(0, tc, issue, 0)
    pltpu.make_async_copy(ys_ref.at[pl.ds(0, TOP_K * tc), :], gbuf, sem).wait()

    acc = x1_ref[...]
    for k in range(TOP_K):
        gate = jnp.concatenate([gexp_ref[:, k * LANES:(k + 1) * LANES]] * (D_MODEL // LANES), axis=1)
        acc = acc + gate * gbuf[pl.ds(k * tc, tc), :]
    o_ref[...] = acc


def _moe_combine(x1, gexp, dest, ys, *, row0, n_rows, tc):
    T = x1.shape[0]
    i0 = row0 // tc
    return pl.pallas_call(
        functools.partial(_combine_kernel, tc=tc),
        grid=(n_rows // tc,),
        in_specs=[
            pl.BlockSpec((1, 1, tc * TOP_K), lambda i: (i0 + i, 0, 0)),
            pl.BlockSpec((tc, D_MODEL), lambda i: (i0 + i, 0)),
            pl.BlockSpec((tc, TOP_K * LANES), lambda i: (i0 + i, 0)),
            pl.BlockSpec(memory_space=pl.ANY),
        ],
        out_specs=pl.BlockSpec((tc, D_MODEL), lambda i: (i, 0)),
        out_shape=jax.ShapeDtypeStruct((n_rows, D_MODEL), F32),
        scratch_shapes=[pltpu.SMEM((tc * TOP_K,), jnp.int32),
                        pltpu.VMEM((tc * TOP_K, D_MODEL), F32),
                        pltpu.SemaphoreType.DMA(()), pltpu.SemaphoreType.DMA(())],
        compiler_params=_cparams(("arbitrary",)),
        name="moe_combine",
    )(dest.reshape(T // tc, 1, tc * TOP_K), x1, gexp, ys)


@functools.lru_cache(maxsize=None)
def _rope_tables_np(n_pos):
    inv = (np.float32(1.0) / np.power(np.float32(ROPE_THETA),
                                      np.arange(0, HEAD_DIM, 2, dtype=np.float32) / np.float32(HEAD_DIM)))
    ang = np.arange(n_pos, dtype=np.float32)[:, None] * inv.astype(np.float32)[None, :]
    ang = ang.astype(np.float32).astype(np.float64)
    return np.cos(ang).astype(np.float32), np.sin(ang).astype(np.float32)


def _rope_tables(n_pos):
    c, s = _rope_tables_np(n_pos)
    return jnp.tile(jnp.asarray(c), (1, 4)), jnp.tile(jnp.asarray(s), (1, 4))


def _layer(x_pair, l, s_p, n_smp, s_s, norm1_g, w_in, q_norm_g, k_norm_g, lambda_q1, lambda_k1,
           lambda_q2, lambda_k2, sub_norm_g, w_fourier, b_fourier, w_out, norm2_g,
           w_router, b_router, w1, b1, w2, b2):
    T = x_pair[0].shape[0] + x_pair[1].shape[0]
    lam_init = 0.8 - 0.6 * math.exp(-0.3 * l)
    tm = min(512, s_s)

    cos, sin = _rope_tables(max(s_p, s_s))
    u, q, k, v = _inproj(
        *x_pair, norm1_g.reshape(1, D_MODEL), w_in.astype(BF16), cos, sin,
        jnp.tile(q_norm_g, 2).reshape(1, LANES), jnp.tile(k_norm_g, 2).reshape(1, LANES),
        tm=tm, n_p=s_p // tm, n_s=s_s // tm)

    w_f_b = w_fourier.astype(BF16)
    u3 = u.reshape(T // DFT_N2, DFT_N2, F_WIDTH)
    f_pair = (_fourier(u3, w_f_b, b_fourier, row0=0, n_seq=1, seq_len=s_p),
              _fourier(u3, w_f_b, b_fourier, row0=s_p, n_seq=n_smp, seq_len=s_s))

    lam_p = jnp.stack([lambda_q1, lambda_k1, lambda_q2, lambda_k2]).astype(F32)
    sub_g = sub_norm_g.reshape(1, V_DIM)
    score_bound = (1.01 * math.sqrt(HEAD_DIM)) * jnp.max(jnp.abs(q_norm_g)) * jnp.max(jnp.abs(k_norm_g))
    bounded = (score_bound <= SCORE_BOUND_NO_MAX).astype(jnp.int32).reshape(1)
    a_pair = (_attention(q, k, v, bounded, lam_p, sub_g, row0=0, n_seq=1, seq_len=s_p,
                         lam_init=lam_init, tq=min(ATTN_TQ, s_p), tk=min(ATTN_TK, s_p)),
              _attention(q, k, v, bounded, lam_p, sub_g, row0=s_p, n_seq=n_smp, seq_len=s_s,
                         lam_init=lam_init, tq=min(ATTN_TQ, s_s), tk=min(ATTN_TK, s_s)))

    x1, h2, top_idx, gexp = _outproj(
        f_pair, a_pair, x_pair, w_out.astype(BF16), norm2_g.reshape(1, D_MODEL), w_router.astype(BF16),
        b_router.reshape(1, N_EXPERTS), tm=min(256, s_s))

    n_blk = T * TOP_K // MOE_TM + N_EXPERTS
    dest, blk_e, blk_src, blk_used = _route(top_idx, n_blk)
    xs = _moe_scatter(h2, dest, n_blk * MOE_TM, te=min(256, s_s))
    w1g, w1l = _w1_split(w1)
    ys = _moe_ffn(
        xs, blk_e, blk_src, blk_used, w1g, w1l,
        b1[:, 0::2].reshape(N_EXPERTS, 1, D_FF), b1[:, 1::2].reshape(N_EXPERTS, 1, D_FF),
        w2.astype(BF16), b2.reshape(N_EXPERTS, 1, D_MODEL))
    tc = min(128, s_s)
    return (_moe_combine(x1, gexp, dest, ys, row0=0, n_rows=s_p, tc=tc),
            _moe_combine(x1, gexp, dest, ys, row0=s_p, n_rows=T - s_p, tc=tc))


def kernel(x_prompt, x_sample, norm1_g, w_in, q_norm_g, k_norm_g, lambda_q1, lambda_k1, lambda_q2,
           lambda_k2, sub_norm_g, w_fourier, b_fourier, w_out, norm2_g, w_router, b_router,
           w1, b1, w2, b2):
    b_p, s_p, d = x_prompt.shape
    n_smp, s_s, _ = x_sample.shape
    assert b_p == 1 and d == D_MODEL
    x_pair = (x_prompt.reshape(s_p, d), x_sample.reshape(n_smp * s_s, d))
    for l in range(norm1_g.shape[0]):
        x_pair = _layer(x_pair, l, s_p, n_smp, s_s, norm1_g[l], w_in[l], q_norm_g[l], k_norm_g[l],
                        lambda_q1[l], lambda_k1[l], lambda_q2[l], lambda_k2[l], sub_norm_g[l],
                        w_fourier[l], b_fourier[l], w_out[l], norm2_g[l], w_router[l], b_router[l],
                        w1[l], b1[l], w2[l], b2[l])
    return (x_pair[0].reshape(1, s_p, d), x_pair[1].reshape(n_smp, s_s, d))
```

```python
import functools
import math

import jax
import jax.numpy as jnp
import numpy as np
from jax import lax
from jax.experimental import pallas as pl
from jax.experimental.pallas import tpu as pltpu

F32 = jnp.float32
BF16 = jnp.bfloat16

D_MODEL = 2048
F_WIDTH = 1024
N_FG = 8
FG_DIM = 128
N_HEADS = 8
HEAD_DIM = 64
V_DIM = 128
QK_WIDTH = 1024
N_EXPERTS = 32
TOP_K = 4
D_FF = 2048
ROPE_THETA = 10000.0
SWIGLU_ALPHA = 1.702
SWIGLU_LIMIT = 7.0
EPS = 1e-5

LANES = 128
HALF_D = D_MODEL // 2
DFT_N2 = 128
VMEM_LIMIT = 48 * 1024 * 1024

MOE_TM = 512
MOE_TF = 1024
FILL_SIZES = tuple(MOE_TM >> i for i in range(7))
ATTN_TQ = 512
ATTN_TK = 512
ATTN_UNROLL = 16
SCORE_BOUND_NO_MAX = 40.0


def _cparams(sem):
    return pltpu.CompilerParams(dimension_semantics=sem, vmem_limit_bytes=VMEM_LIMIT)


def _norm_rope_store(acc, g_ref, cos_ref, sin_ref, o_ref, scale):
    tm = acc.shape[0]
    w = 2 * LANES
    r = lax.broadcasted_iota(jnp.int32, (w, w), 0) // HEAD_DIM
    c = lax.broadcasted_iota(jnp.int32, (w, w), 1) // HEAD_DIM
    group_mean = jnp.where(r == c, 1.0 / HEAD_DIM, 0.0).astype(BF16)
    lane = lax.broadcasted_iota(jnp.int32, (tm, w), 1)
    first_half = (lane % HEAD_DIM) < (HEAD_DIM // 2)
    cos = jnp.concatenate([cos_ref[...]] * 2, axis=1)
    sin = jnp.concatenate([sin_ref[...]] * 2, axis=1)
    g = jnp.concatenate([g_ref[...]] * 2, axis=1)
    for h in range(acc.shape[1] // w):
        a = acc[:, h * w:(h + 1) * w]
        sq = a * a
        hi = sq.astype(BF16)
        lo = (sq - hi.astype(F32)).astype(BF16)
        ms = (jnp.dot(hi, group_mean, preferred_element_type=F32)
              + jnp.dot(lo, group_mean, preferred_element_type=F32))
        y = a * lax.rsqrt(ms + EPS) * g
        rot = jnp.where(first_half, -pltpu.roll(y, w - HEAD_DIM // 2, 1),
                        pltpu.roll(y, HEAD_DIM // 2, 1))
        o_ref[:, h * w:(h + 1) * w] = ((y * cos + rot * sin) * scale).astype(BF16)


def _pick(n_first, first_ref, second_ref):
    return jnp.where(pl.program_id(0) < n_first, first_ref[...], second_ref[...])


def _pair_specs(tm, width, n_first, n_grid_axes):
    if n_grid_axes == 1:
        return [pl.BlockSpec((tm, width), lambda i: (jnp.minimum(i, n_first - 1), 0)),
                pl.BlockSpec((tm, width), lambda i: (jnp.maximum(i - n_first, 0), 0))]
    return [pl.BlockSpec((tm, width), lambda i, j: (jnp.minimum(i, n_first - 1), 0)),
            pl.BlockSpec((tm, width), lambda i, j: (jnp.maximum(i - n_first, 0), 0))]


def _inproj_kernel(xp_ref, xs_ref, g1_ref, w_ref, cos_ref, sin_ref, gq_ref, gk_ref,
                   u_ref, q_ref, k_ref, v_ref, h_ref, *, n_p):
    j = pl.program_id(1)

    @pl.when(j == 0)
    def _():
        x = _pick(n_p, xp_ref, xs_ref)
        ms = jnp.mean(x * x, axis=-1, keepdims=True)
        h_ref[...] = (x * lax.rsqrt(ms + EPS) * g1_ref[...]).astype(BF16)

    acc = jnp.dot(h_ref[...], w_ref[...], preferred_element_type=F32)

    @pl.when(j == 0)
    def _():
        u_ref[...] = acc

    @pl.when(j == 1)
    def _():
        _norm_rope_store(acc, gq_ref, cos_ref, sin_ref, q_ref, 1.0 / math.sqrt(HEAD_DIM))

    @pl.when(j == 2)
    def _():
        _norm_rope_store(acc, gk_ref, cos_ref, sin_ref, k_ref, 1.0)

    @pl.when(j == 3)
    def _():
        v_ref[...] = acc.astype(BF16)


def _inproj(xp, xs, g1, w_in_b, cos, sin, gq, gk, *, tm, n_p, n_s):
    T = xp.shape[0] + xs.shape[0]

    def pos_blk(i, j):
        return (jnp.where(i < n_p, i, (i - n_p) % n_s), 0)

    out = jax.ShapeDtypeStruct((T, 1024), BF16)
    out_u = jax.ShapeDtypeStruct((T, 1024), F32)
    return pl.pallas_call(
        functools.partial(_inproj_kernel, n_p=n_p),
        grid=(T // tm, 4),
        in_specs=_pair_specs(tm, D_MODEL, n_p, 2) + [
            pl.BlockSpec((1, D_MODEL), lambda i, j: (0, 0)),
            pl.BlockSpec((D_MODEL, 1024), lambda i, j: (0, j)),
            pl.BlockSpec((tm, LANES), pos_blk),
            pl.BlockSpec((tm, LANES), pos_blk),
            pl.BlockSpec((1, LANES), lambda i, j: (0, 0)),
            pl.BlockSpec((1, LANES), lambda i, j: (0, 0)),
        ],
        out_specs=[pl.BlockSpec((tm, 1024), lambda i, j: (i, 0))] * 4,
        out_shape=[out_u, out, out, out],
        scratch_shapes=[pltpu.VMEM((tm, D_MODEL), BF16)],
        compiler_params=_cparams(("parallel", "arbitrary")),
        name="inproj",
    )(xp, xs, g1, w_in_b, cos, sin, gq, gk)


def _fft1_kernel(x_ref, c1_ref, s1_ref, twc_ref, tws_ref, tr_ref, ti_ref):
    c1 = c1_ref[...]
    s1 = s1_ref[...]
    for t in range(x_ref.shape[1]):
        x = x_ref[:, t, :].astype(BF16)
        a = jnp.dot(c1, x, preferred_element_type=F32)
        b = -jnp.dot(s1, x, preferred_element_type=F32)
        c = jnp.concatenate([twc_ref[t]] * N_FG, axis=1)
        s = jnp.concatenate([tws_ref[t]] * N_FG, axis=1)
        tr_ref[t] = a * c + b * s
        ti_ref[t] = b * c - a * s


def _fft2_kernel(tr_ref, ti_ref, c2x_ref, s2x_ref, cc_ref, sc_ref, wf_ref, bf_ref, o_ref, *, scale):
    n = tr_ref.shape[0] * tr_ref.shape[1]
    tr = tr_ref[...].reshape(n, F_WIDTH).astype(BF16)
    ti = ti_ref[...].reshape(n, F_WIDTH).astype(BF16)
    cc = cc_ref[...]
    sc = sc_ref[...]
    for i in range(tr_ref.shape[1]):
        c2 = c2x_ref[i]
        s2 = s2x_ref[i]
        ur = (jnp.dot(c2, tr, preferred_element_type=F32) + jnp.dot(s2, ti, preferred_element_type=F32))
        ui = (jnp.dot(c2, ti, preferred_element_type=F32) - jnp.dot(s2, tr, preferred_element_type=F32))
        outs = []
        for g in range(N_FG):
            sl = slice(g * FG_DIM, (g + 1) * FG_DIM)
            y = (jnp.dot(ur[:, sl].astype(BF16), cc, preferred_element_type=F32)
                 + jnp.dot(ui[:, sl].astype(BF16), sc, preferred_element_type=F32))
            y = (y * scale).astype(BF16)
            outs.append(jnp.dot(y, wf_ref[g], preferred_element_type=F32) + bf_ref[g])
        o_ref[:, i, :] = jnp.concatenate(outs, axis=1)


@functools.lru_cache(maxsize=None)
def _dft_mats_np(n):
    j = np.arange(n)
    ang = (2.0 * np.pi / n) * ((j[:, None] * j[None, :]) % n)
    return np.cos(ang).astype(np.float32), np.sin(ang).astype(np.float32)


def _dft_mats(n):
    c, s = _dft_mats_np(n)
    return jnp.asarray(c).astype(BF16), jnp.asarray(s).astype(BF16)


@functools.lru_cache(maxsize=None)
def _spread_dft_np(tk1):
    c, s = _dft_mats_np(DFT_N2)
    cx = np.zeros((tk1, DFT_N2, DFT_N2 * tk1), np.float32)
    sx = np.zeros((tk1, DFT_N2, DFT_N2 * tk1), np.float32)
    for i in range(tk1):
        cx[i, :, i::tk1] = c
        sx[i, :, i::tk1] = s
    return cx, sx


@functools.lru_cache(maxsize=None)
def _twiddle_np(n1, seq_len):
    ang = (2.0 * np.pi / seq_len) * (np.arange(DFT_N2)[:, None] * np.arange(n1)[None, :])
    return np.cos(ang).astype(np.float32), np.sin(ang).astype(np.float32)


def _fourier(u3, w_f_b, b_f, *, row0, n_seq, seq_len):
    n1 = seq_len // DFT_N2
    blk0 = row0 // seq_len
    c1, s1 = _dft_mats(n1)
    c2, s2 = _dft_mats(DFT_N2)
    tw_cos, tw_sin = _twiddle_np(n1, seq_len)
    twc = jnp.broadcast_to(jnp.asarray(tw_cos)[:, :, None], (DFT_N2, n1, LANES))
    tws = jnp.broadcast_to(jnp.asarray(tw_sin)[:, :, None], (DFT_N2, n1, LANES))

    tb = 8
    t_shape = jax.ShapeDtypeStruct((n_seq, DFT_N2, n1, F_WIDTH), F32)
    tr, ti = pl.pallas_call(
        _fft1_kernel,
        grid=(n_seq, DFT_N2 // tb),
        in_specs=[
            pl.BlockSpec((n1, tb, F_WIDTH), lambda b, j: (blk0 + b, j, 0)),
            pl.BlockSpec((n1, n1), lambda b, j: (0, 0)),
            pl.BlockSpec((n1, n1), lambda b, j: (0, 0)),
            pl.BlockSpec((tb, n1, LANES), lambda b, j: (j, 0, 0)),
            pl.BlockSpec((tb, n1, LANES), lambda b, j: (j, 0, 0)),
        ],
        out_specs=[pl.BlockSpec((None, tb, n1, F_WIDTH), lambda b, j: (b, j, 0, 0))] * 2,
        out_shape=[t_shape] * 2,
        compiler_params=_cparams(("parallel", "parallel")),
        name="fft_stage1",
    )(u3, c1, s1, twc, tws)

    tk1 = min(8, n1)
    c2x, s2x = (jnp.asarray(m).astype(BF16) for m in _spread_dft_np(tk1))
    scale = 1.0 / math.sqrt(seq_len * FG_DIM)
    const = lambda b, j: (0, 0)
    f = pl.pallas_call(
        functools.partial(_fft2_kernel, scale=scale),
        grid=(n_seq, n1 // tk1),
        in_specs=[
            pl.BlockSpec((None, DFT_N2, tk1, F_WIDTH), lambda b, j: (b, 0, j, 0)),
            pl.BlockSpec((None, DFT_N2, tk1, F_WIDTH), lambda b, j: (b, 0, j, 0)),
            pl.BlockSpec((tk1, DFT_N2, DFT_N2 * tk1), lambda b, j: (0, 0, 0)),
            pl.BlockSpec((tk1, DFT_N2, DFT_N2 * tk1), lambda b, j: (0, 0, 0)),
            pl.BlockSpec((FG_DIM, FG_DIM), const),
            pl.BlockSpec((FG_DIM, FG_DIM), const),
            pl.BlockSpec((N_FG, FG_DIM, FG_DIM), lambda b, j: (0, 0, 0)),
            pl.BlockSpec((N_FG, 1, FG_DIM), lambda b, j: (0, 0, 0)),
        ],
        out_specs=pl.BlockSpec((None, DFT_N2, tk1, F_WIDTH), lambda b, j: (b, 0, j, 0)),
        out_shape=jax.ShapeDtypeStruct((n_seq, DFT_N2, n1, F_WIDTH), F32),
        compiler_params=_cparams(("parallel", "parallel")),
        name="fft_stage2",
    )(tr, ti, c2x, s2x, c2, s2, w_f_b, b_f.reshape(N_FG, 1, FG_DIM))
    return f.reshape(n_seq * seq_len, F_WIDTH)


def _attn_finalize(lam_ref, subg_ref, o_ref, acc0, acc1, l0, l1, lam_init):
    lam_p = lam_ref[...]
    lam = (jnp.exp(jnp.sum(lam_p[0:1] * lam_p[1:2], axis=1, keepdims=True))
           - jnp.exp(jnp.sum(lam_p[2:3] * lam_p[3:4], axis=1, keepdims=True)) + lam_init)
    o = acc0 / l0 - lam * (acc1 / l1)
    ms_o = jnp.mean(o * o, axis=-1, keepdims=True)
    o = o * lax.rsqrt(ms_o + EPS) * subg_ref[...]
    o_ref[...] = (o * (1.0 - lam_init)).astype(BF16)


def _attn_kernel(bounded_ref, lam_ref, subg_ref, q_ref, k_ref, v_ref, o_ref,
                 q2_ref, acc_ref, m_ref, l_ref, ls_ref, *, seq_len, tk, lam_init):
    tq = q_ref.shape[0]
    q = q_ref[...]
    lane = lax.broadcasted_iota(jnp.int32, q.shape, 1)
    zero = jnp.zeros_like(q)
    q2_ref[0:tq, :] = jnp.where(lane < HEAD_DIM, q, zero)
    q2_ref[tq:2 * tq, :] = jnp.where(lane >= HEAD_DIM, q, zero)
    acc_ref[...] = jnp.zeros_like(acc_ref)
    nt = (((1,), (1,)), ((), ()))

    def scores(j):
        start = pl.multiple_of(j * tk, tk)
        s = lax.dot_general(q2_ref[...], k_ref[pl.ds(start, tk), :], nt,
                            preferred_element_type=F32)
        return s, v_ref[pl.ds(start, tk), :]

    def finalize(l):
        acc = acc_ref[...]
        _attn_finalize(lam_ref, subg_ref, o_ref, acc[0:tq], acc[tq:2 * tq],
                       l[0:tq], l[tq:2 * tq], lam_init)

    @pl.when(bounded_ref[0] == 1)
    def _():
        ls_ref[...] = jnp.zeros_like(ls_ref)

        def body(j, carry):
            s, vt = scores(j)
            p = jnp.exp(s)
            part = p[:, 0:LANES]
            for t in range(1, tk // LANES):
                part = part + p[:, t * LANES:(t + 1) * LANES]
            ls_ref[...] += part
            acc_ref[...] += jnp.dot(p.astype(BF16), vt, preferred_element_type=F32)
            return carry

        lax.fori_loop(0, seq_len // tk, body, 0, unroll=ATTN_UNROLL)
        finalize(jnp.sum(ls_ref[...], axis=1, keepdims=True))

    @pl.when(bounded_ref[0] == 0)
    def _():
        m_ref[...] = jnp.full_like(m_ref, -jnp.inf)
        l_ref[...] = jnp.zeros_like(l_ref)

        def body(j, carry):
            s, vt = scores(j)
            m_prev = m_ref[...]
            m_new = jnp.maximum(m_prev, jnp.max(s, axis=1, keepdims=True))
            alpha = jnp.exp(m_prev - m_new)
            p = jnp.exp(s - m_new)
            l_ref[...] = alpha * l_ref[...] + jnp.sum(p, axis=1, keepdims=True)
            acc_ref[...] = alpha * acc_ref[...] + jnp.dot(p.astype(BF16), vt,
                                                          preferred_element_type=F32)
            m_ref[...] = m_new
            return carry

        lax.fori_loop(0, seq_len // tk, body, 0)
        finalize(l_ref[...])


def _attention(q, k, v, bounded, lam_p, sub_g, *, row0, n_seq, seq_len, lam_init, tq, tk):
    nq = seq_len // tq
    q0 = row0 // tq
    kv0 = row0 // seq_len
    kernel = functools.partial(_attn_kernel, seq_len=seq_len, tk=tk, lam_init=lam_init)
    grid_spec = pltpu.PrefetchScalarGridSpec(
        num_scalar_prefetch=1,
        grid=(n_seq, N_HEADS, nq),
        in_specs=[
            pl.BlockSpec((4, HEAD_DIM), lambda b, h, i, f: (0, 0)),
            pl.BlockSpec((1, V_DIM), lambda b, h, i, f: (0, 0)),
            pl.BlockSpec((tq, LANES), lambda b, h, i, f: (q0 + b * nq + i, h)),
            pl.BlockSpec((seq_len, LANES), lambda b, h, i, f: (kv0 + b, h)),
            pl.BlockSpec((seq_len, LANES), lambda b, h, i, f: (kv0 + b, h)),
        ],
        out_specs=pl.BlockSpec((tq, V_DIM), lambda b, h, i, f: (b * nq + i, h)),
        scratch_shapes=[pltpu.VMEM((2 * tq, LANES), BF16), pltpu.VMEM((2 * tq, V_DIM), F32),
                        pltpu.VMEM((2 * tq, 1), F32), pltpu.VMEM((2 * tq, 1), F32),
                        pltpu.VMEM((2 * tq, LANES), F32)],
    )
    return pl.pallas_call(
        kernel,
        grid_spec=grid_spec,
        out_shape=jax.ShapeDtypeStruct((n_seq * seq_len, N_HEADS * V_DIM), BF16),
        compiler_params=_cparams(("parallel", "parallel", "arbitrary")),
        name="diff_attn",
    )(bounded, lam_p, sub_g, q, k, v)


def _outproj_kernel(fp_ref, fs_ref, ap_ref, as_ref, xp_ref, xs_ref, wo_ref, g2_ref, wr_ref, br_ref,
                    x1_ref, h2_ref, idx_ref, gexp_ref, *, n_p):
    acc = (jnp.dot(_pick(n_p, fp_ref, fs_ref).astype(BF16), wo_ref[0:F_WIDTH, :],
                   preferred_element_type=F32)
           + jnp.dot(_pick(n_p, ap_ref, as_ref), wo_ref[F_WIDTH:D_MODEL, :],
                     preferred_element_type=F32))
    x1 = _pick(n_p, xp_ref, xs_ref) + acc
    x1_ref[...] = x1
    ms = jnp.mean(x1 * x1, axis=-1, keepdims=True)
    h2 = (x1 * lax.rsqrt(ms + EPS) * g2_ref[...]).astype(BF16)
    lo = lax.bitcast_convert_type(h2[:, 0:HALF_D].astype(F32), jnp.uint32)
    hi = lax.bitcast_convert_type(h2[:, HALF_D:D_MODEL].astype(F32), jnp.uint32)
    h2_ref[...] = lax.shift_right_logical(lo, jnp.uint32(16)) | (hi & jnp.uint32(0xFFFF0000))
    logits = jnp.dot(h2, wr_ref[...], preferred_element_type=F32) + br_ref[...]
    tm = logits.shape[0]
    lane = lax.broadcasted_iota(jnp.int32, (tm, N_EXPERTS), 1)
    cur = logits
    vals, idxs = [], []
    for _ in range(TOP_K):
        m = jnp.max(cur, axis=1, keepdims=True)
        am = jnp.min(jnp.where(cur == m, lane, N_EXPERTS), axis=1, keepdims=True)
        vals.append(m)
        idxs.append(am)
        cur = jnp.where(lane == am, -jnp.inf, cur)
    idx_ref[...] = jnp.concatenate(idxs, axis=1)
    es = [jnp.exp(v - vals[0]) for v in vals]
    den = es[0] + es[1] + es[2] + es[3]
    gexp_ref[...] = jnp.concatenate([jnp.broadcast_to(e / den, (tm, LANES)) for e in es], axis=1)


def _outproj(f_pair, a_pair, x_pair, w_out_b, g2, w_r_b, b_r, *, tm):
    T = x_pair[0].shape[0] + x_pair[1].shape[0]
    n_p = x_pair[0].shape[0] // tm
    const = lambda i: (0, 0)
    return pl.pallas_call(
        functools.partial(_outproj_kernel, n_p=n_p),
        grid=(T // tm,),
        in_specs=_pair_specs(tm, F_WIDTH, n_p, 1) + _pair_specs(tm, F_WIDTH, n_p, 1)
        + _pair_specs(tm, D_MODEL, n_p, 1) + [
            pl.BlockSpec((D_MODEL, D_MODEL), const),
            pl.BlockSpec((1, D_MODEL), const),
            pl.BlockSpec((D_MODEL, N_EXPERTS), const),
            pl.BlockSpec((1, N_EXPERTS), const),
        ],
        out_specs=[
            pl.BlockSpec((tm, D_MODEL), lambda i: (i, 0)),
            pl.BlockSpec((tm, HALF_D), lambda i: (i, 0)),
            pl.BlockSpec((tm, TOP_K), lambda i: (i, 0)),
            pl.BlockSpec((tm, TOP_K * LANES), lambda i: (i, 0)),
        ],
        out_shape=[
            jax.ShapeDtypeStruct((T, D_MODEL), F32),
            jax.ShapeDtypeStruct((T, HALF_D), jnp.uint32),
            jax.ShapeDtypeStruct((T, TOP_K), jnp.int32),
            jax.ShapeDtypeStruct((T, TOP_K * LANES), F32),
        ],
        compiler_params=_cparams(("parallel",)),
        name="outproj_router",
    )(*f_pair, *a_pair, *x_pair, w_out_b, g2, w_r_b, b_r)


def _route(top_idx, n_blk):
    flat_e = top_idx.reshape(-1)
    onehot = (flat_e[:, None] == jnp.arange(N_EXPERTS, dtype=jnp.int32)[None, :]).astype(jnp.int32)
    csum = jnp.cumsum(onehot, axis=0)
    counts = csum[-1]
    rank = jnp.sum(csum * onehot, axis=1) - 1
    nblk_e = (counts + MOE_TM - 1) // MOE_TM
    blk_end = jnp.cumsum(nblk_e)
    blk_start = blk_end - nblk_e
    dest = jnp.sum(onehot * (blk_start * MOE_TM)[None, :], axis=1) + rank
    b = jnp.arange(n_blk, dtype=jnp.int32)
    n_used = blk_end[-1]
    blk_src = jnp.minimum(b, n_used - 1)
    blk_e = jnp.minimum(jnp.sum((blk_end[None, :] <= blk_src[:, None]).astype(jnp.int32), axis=1),
                        N_EXPERTS - 1)
    blk_used = (b < n_used).astype(jnp.int32)
    pad_start = (blk_start * MOE_TM + counts) // 8 * 8
    pad_len = blk_end * MOE_TM - pad_start
    fill = jnp.concatenate([pad_start, pad_len, n_used[None]]).astype(jnp.int32)
    return dest.astype(jnp.int32), blk_e.astype(jnp.int32), blk_src.astype(jnp.int32), blk_used, fill


def _w1_split_kernel(w_ref, perm_ref, g_ref, l_ref):
    perm = perm_ref[...]
    half = perm.shape[0] // 2
    for c in range(w_ref.shape[2] // perm.shape[0]):
        w = w_ref[0, :, c * 2 * half:(c + 1) * 2 * half].astype(BF16)
        r = jnp.dot(w, perm, preferred_element_type=F32)
        g_ref[0, :, c * half:(c + 1) * half] = r[:, :half].astype(BF16)
        l_ref[0, :, c * half:(c + 1) * half] = r[:, half:].astype(BF16)


def _w1_split(w1):
    n_e, d, f2 = w1.shape
    chunk = 2 * LANES
    tcol = 512
    i = jnp.arange(chunk, dtype=jnp.int32)
    src = jnp.where(i < LANES, 2 * i, 2 * (i - LANES) + 1)
    perm = (i[:, None] == src[None, :]).astype(BF16)
    out = jax.ShapeDtypeStruct((n_e, d, f2 // 2), BF16)
    return pl.pallas_call(
        _w1_split_kernel,
        grid=(n_e, f2 // tcol),
        in_specs=[pl.BlockSpec((1, d, tcol), lambda e, j: (e, 0, j)),
                  pl.BlockSpec((chunk, chunk), lambda e, j: (0, 0))],
        out_specs=[pl.BlockSpec((1, d, tcol // 2), lambda e, j: (e, 0, j))] * 2,
        out_shape=[out, out],
        compiler_params=_cparams(("parallel", "parallel")),
        name="w1_split",
    )(w1, perm)


def _scatter_kernel(fill_ref, dest_ref, h_ref, xs_ref, idx_smem, zero_ref, idx_sem, sem, zero_sem,
                    *, te, n_blk):
    @pl.when(pl.program_id(0) == 0)
    def _():
        zero_ref[...] = jnp.zeros_like(zero_ref)

        def fill_copy(row, size):
            return pltpu.make_async_copy(zero_ref.at[pl.ds(0, size), :],
                                         xs_ref.at[pl.ds(row, size), :], zero_sem)

        def expert_fills(act):
            for e in range(N_EXPERTS):
                row = fill_ref[e]
                for size in FILL_SIZES:
                    take = (fill_ref[N_EXPERTS + e] & size) != 0

                    @pl.when(take)
                    def _(row=row, size=size):
                        act(fill_copy(pl.multiple_of(row, 8), size))

                    row = row + jnp.where(take, size, 0)

        def block_fills(act):
            def one(b, carry):
                act(fill_copy(pl.multiple_of(b * MOE_TM, MOE_TM), MOE_TM))
                return carry

            lax.fori_loop(fill_ref[2 * N_EXPERTS], n_blk, one, 0)

        expert_fills(lambda cp: cp.start())
        block_fills(lambda cp: cp.start())
        expert_fills(lambda cp: cp.wait())
        block_fills(lambda cp: cp.wait())

    cp = pltpu.make_async_copy(dest_ref.at[0, 0], idx_smem, idx_sem)
    cp.start()
    cp.wait()

    def row_copy(t, d):
        return pltpu.make_async_copy(h_ref.at[pl.ds(t, 1), :], xs_ref.at[pl.ds(d, 1), :], sem)

    def issue(t, carry):
        for k in range(TOP_K):
            row_copy(t, idx_smem[TOP_K * t + k]).start()
        return carry

    lax.fori_loop(0, te, issue, 0)
    for k in range(TOP_K):
        pltpu.make_async_copy(h_ref, xs_ref.at[pl.ds(0, te), :], sem).wait()


def _moe_scatter(h2p, dest, fill, n_rows, *, te):
    T = h2p.shape[0]
    dest3 = dest.reshape(T // te, 1, te * TOP_K)
    n_blk = n_rows // MOE_TM
    grid_spec = pltpu.PrefetchScalarGridSpec(
        num_scalar_prefetch=1,
        grid=(T // te,),
        in_specs=[
            pl.BlockSpec((1, 1, te * TOP_K), lambda i, fl: (i, 0, 0)),
            pl.BlockSpec((te, HALF_D), lambda i, fl: (i, 0)),
        ],
        out_specs=pl.BlockSpec(memory_space=pl.ANY),
        scratch_shapes=[pltpu.SMEM((te * TOP_K,), jnp.int32),
                        pltpu.VMEM((MOE_TM, HALF_D), jnp.uint32),
                        pltpu.SemaphoreType.DMA(()), pltpu.SemaphoreType.DMA(()),
                        pltpu.SemaphoreType.DMA(())],
    )
    return pl.pallas_call(
        functools.partial(_scatter_kernel, te=te, n_blk=n_blk),
        grid_spec=grid_spec,
        out_shape=jax.ShapeDtypeStruct((n_rows, HALF_D), jnp.uint32),
        compiler_params=_cparams(("arbitrary",)),
        name="moe_scatter",
    )(fill, dest3, h2p)


def _moe_ffn_kernel(be_ref, bsrc_ref, bused_ref, x_ref, w1g_ref, w1l_ref, b1g_ref, b1l_ref,
                    w2_ref, b2_ref, o_ref, xb_ref):
    b = pl.program_id(0)
    f = pl.program_id(1)

    @pl.when(jnp.logical_and(bused_ref[b] > 0, f == 0))
    def _():
        w = x_ref[...]
        lo = lax.bitcast_convert_type(lax.shift_left(w, jnp.uint32(16)), F32)
        hi = lax.bitcast_convert_type(w & jnp.uint32(0xFFFF0000), F32)
        xb_ref[:, 0:HALF_D] = lo.astype(BF16)
        xb_ref[:, HALF_D:D_MODEL] = hi.astype(BF16)

    @pl.when(bused_ref[b] > 0)
    def _():
        x = xb_ref[...]
        hg = jnp.dot(x, w1g_ref[0], preferred_element_type=F32) + b1g_ref[0]
        hl = jnp.dot(x, w1l_ref[0], preferred_element_type=F32) + b1l_ref[0]
        hg = jnp.minimum(hg, SWIGLU_LIMIT)
        hl = jnp.clip(hl, -SWIGLU_LIMIT, SWIGLU_LIMIT)
        act = hg * jax.nn.sigmoid(SWIGLU_ALPHA * hg) * (hl + 1.0)
        part = jnp.dot(act.astype(BF16), w2_ref[0], preferred_element_type=F32)

        @pl.when(f == 0)
        def _():
            o_ref[...] = part + b2_ref[0]

        @pl.when(f > 0)
        def _():
            o_ref[...] += part

    @pl.when(jnp.logical_and(bused_ref[b] == 0, f == 0))
    def _():
        o_ref[...] = jnp.zeros_like(o_ref)


def _moe_ffn(xs, blk_e, blk_src, blk_used, w1g, w1l, b1g, b1l, w2b, b2):
    n_rows = xs.shape[0]
    n_blk = n_rows // MOE_TM
    n_f = D_FF // MOE_TF

    def f_idx(f, used):
        return jnp.where(used > 0, f, n_f - 1)

    grid_spec = pltpu.PrefetchScalarGridSpec(
        num_scalar_prefetch=3,
        grid=(n_blk, n_f),
        in_specs=[
            pl.BlockSpec((MOE_TM, HALF_D), lambda b, f, be, bs, bu: (bs[b], 0)),
            pl.BlockSpec((1, D_MODEL, MOE_TF), lambda b, f, be, bs, bu: (be[b], 0, f_idx(f, bu[b]))),
            pl.BlockSpec((1, D_MODEL, MOE_TF), lambda b, f, be, bs, bu: (be[b], 0, f_idx(f, bu[b]))),
            pl.BlockSpec((1, 1, MOE_TF), lambda b, f, be, bs, bu: (be[b], 0, f_idx(f, bu[b]))),
            pl.BlockSpec((1, 1, MOE_TF), lambda b, f, be, bs, bu: (be[b], 0, f_idx(f, bu[b]))),
            pl.BlockSpec((1, MOE_TF, D_MODEL), lambda b, f, be, bs, bu: (be[b], f_idx(f, bu[b]), 0)),
            pl.BlockSpec((1, 1, D_MODEL), lambda b, f, be, bs, bu: (be[b], 0, 0)),
        ],
        out_specs=pl.BlockSpec((MOE_TM, D_MODEL), lambda b, f, be, bs, bu: (b, 0)),
        scratch_shapes=[pltpu.VMEM((MOE_TM, D_MODEL), BF16)],
    )
    return pl.pallas_call(
        _moe_ffn_kernel,
        grid_spec=grid_spec,
        out_shape=jax.ShapeDtypeStruct((n_rows, D_MODEL), F32),
        compiler_params=_cparams(("arbitrary", "arbitrary")),
        name="moe_ffn",
    )(blk_e, blk_src, blk_used, xs, w1g, w1l, b1g, b1l, w2b, b2)


def _combine_kernel(dest_ref, x1_ref, gexp_ref, ys_ref, o_ref, idx_smem, gbuf, idx_sem, sem, *, tc):
    cp = pltpu.make_async_copy(dest_ref.at[0, 0], idx_smem, idx_sem)
    cp.start()
    cp.wait()

    def row_copy(d, r):
        return pltpu.make_async_copy(ys_ref.at[pl.ds(d, 1), :], gbuf.at[pl.ds(r, 1), :], sem)

    def issue(t, carry):
        for k in range(TOP_K):
            row_copy(idx_smem[TOP_K * t + k], k * tc + t).start()
        return carry

    lax.fori_loop(0, tc, issue, 0)
    pltpu.make_async_copy(ys_ref.at[pl.ds(0, TOP_K * tc), :], gbuf, sem).wait()

    acc = x1_ref[...]
    for k in range(TOP_K):
        gate = jnp.concatenate([gexp_ref[:, k * LANES:(k + 1) * LANES]] * (D_MODEL // LANES), axis=1)
        acc = acc + gate * gbuf[pl.ds(k * tc, tc), :]
    o_ref[...] = acc


def _moe_combine(x1, gexp, dest, ys, *, row0, n_rows, tc):
    T = x1.shape[0]
    i0 = row0 // tc
    return pl.pallas_call(
        functools.partial(_combine_kernel, tc=tc),
        grid=(n_rows // tc,),
        in_specs=[
            pl.BlockSpec((1, 1, tc * TOP_K), lambda i: (i0 + i, 0, 0)),
            pl.BlockSpec((tc, D_MODEL), lambda i: (i0 + i, 0)),
            pl.BlockSpec((tc, TOP_K * LANES), lambda i: (i0 + i, 0)),
            pl.BlockSpec(memory_space=pl.ANY),
        ],
        out_specs=pl.BlockSpec((tc, D_MODEL), lambda i: (i, 0)),
        out_shape=jax.ShapeDtypeStruct((n_rows, D_MODEL), F32),
        scratch_shapes=[pltpu.SMEM((tc * TOP_K,), jnp.int32),
                        pltpu.VMEM((tc * TOP_K, D_MODEL), F32),
                        pltpu.SemaphoreType.DMA(()), pltpu.SemaphoreType.DMA(())],
        compiler_params=_cparams(("arbitrary",)),
        name="moe_combine",
    )(dest.reshape(T // tc, 1, tc * TOP_K), x1, gexp, ys)


@functools.lru_cache(maxsize=None)
def _rope_tables_np(n_pos):
    inv = (np.float32(1.0) / np.power(np.float32(ROPE_THETA),
                                      np.arange(0, HEAD_DIM, 2, dtype=np.float32) / np.float32(HEAD_DIM)))
    ang = np.arange(n_pos, dtype=np.float32)[:, None] * inv.astype(np.float32)[None, :]
    ang = ang.astype(np.float32).astype(np.float64)
    return np.cos(ang).astype(np.float32), np.sin(ang).astype(np.float32)


def _rope_tables(n_pos):
    c, s = _rope_tables_np(n_pos)
    return jnp.tile(jnp.asarray(c), (1, 4)), jnp.tile(jnp.asarray(s), (1, 4))


def _layer(x_pair, l, s_p, n_smp, s_s, norm1_g, w_in, q_norm_g, k_norm_g, lambda_q1, lambda_k1,
           lambda_q2, lambda_k2, sub_norm_g, w_fourier, b_fourier, w_out, norm2_g,
           w_router, b_router, w1, b1, w2, b2):
    T = x_pair[0].shape[0] + x_pair[1].shape[0]
    lam_init = 0.8 - 0.6 * math.exp(-0.3 * l)
    tm = min(512, s_s)

    cos, sin = _rope_tables(max(s_p, s_s))
    u, q, k, v = _inproj(
        *x_pair, norm1_g.reshape(1, D_MODEL), w_in.astype(BF16), cos, sin,
        jnp.tile(q_norm_g, 2).reshape(1, LANES), jnp.tile(k_norm_g, 2).reshape(1, LANES),
        tm=tm, n_p=s_p // tm, n_s=s_s // tm)

    w_f_b = w_fourier.astype(BF16)
    u3 = u.reshape(T // DFT_N2, DFT_N2, F_WIDTH)
    f_pair = (_fourier(u3, w_f_b, b_fourier, row0=0, n_seq=1, seq_len=s_p),
              _fourier(u3, w_f_b, b_fourier, row0=s_p, n_seq=n_smp, seq_len=s_s))

    lam_p = jnp.stack([lambda_q1, lambda_k1, lambda_q2, lambda_k2]).astype(F32)
    sub_g = sub_norm_g.reshape(1, V_DIM)
    score_bound = (1.01 * math.sqrt(HEAD_DIM)) * jnp.max(jnp.abs(q_norm_g)) * jnp.max(jnp.abs(k_norm_g))
    bounded = (score_bound <= SCORE_BOUND_NO_MAX).astype(jnp.int32).reshape(1)
    a_pair = (_attention(q, k, v, bounded, lam_p, sub_g, row0=0, n_seq=1, seq_len=s_p,
                         lam_init=lam_init, tq=min(ATTN_TQ, s_p), tk=min(ATTN_TK, s_p)),
              _attention(q, k, v, bounded, lam_p, sub_g, row0=s_p, n_seq=n_smp, seq_len=s_s,
                         lam_init=lam_init, tq=min(ATTN_TQ, s_s), tk=min(ATTN_TK, s_s)))

    x1, h2, top_idx, gexp = _outproj(
        f_pair, a_pair, x_pair, w_out.astype(BF16), norm2_g.reshape(1, D_MODEL), w_router.astype(BF16),
        b_router.reshape(1, N_EXPERTS), tm=min(256, s_s))

    n_blk = T * TOP_K // MOE_TM + N_EXPERTS
    dest, blk_e, blk_src, blk_used, fill = _route(top_idx, n_blk)
    xs = _moe_scatter(h2, dest, fill, n_blk * MOE_TM, te=min(256, s_s))
    w1g, w1l = _w1_split(w1)
    ys = _moe_ffn(
        xs, blk_e, blk_src, blk_used, w1g, w1l,
        b1[:, 0::2].reshape(N_EXPERTS, 1, D_FF), b1[:, 1::2].reshape(N_EXPERTS, 1, D_FF),
        w2.astype(BF16), b2.reshape(N_EXPERTS, 1, D_MODEL))
    tc = min(128, s_s)
    return (_moe_combine(x1, gexp, dest, ys, row0=0, n_rows=s_p, tc=tc),
            _moe_combine(x1, gexp, dest, ys, row0=s_p, n_rows=T - s_p, tc=tc))


def kernel(x_prompt, x_sample, norm1_g, w_in, q_norm_g, k_norm_g, lambda_q1, lambda_k1, lambda_q2,
           lambda_k2, sub_norm_g, w_fourier, b_fourier, w_out, norm2_g, w_router, b_router,
           w1, b1, w2, b2):
    b_p, s_p, d = x_prompt.shape
    n_smp, s_s, _ = x_sample.shape
    assert b_p == 1 and d == D_MODEL
    x_pair = (x_prompt.reshape(s_p, d), x_sample.reshape(n_smp * s_s, d))
    for l in range(norm1_g.shape[0]):
        x_pair = _layer(x_pair, l, s_p, n_smp, s_s, norm1_g[l], w_in[l], q_norm_g[l], k_norm_g[l],
                        lambda_q1[l], lambda_k1[l], lambda_q2[l], lambda_k2[l], sub_norm_g[l],
                        w_fourier[l], b_fourier[l], w_out[l], norm2_g[l], w_router[l], b_router[l],
                        w1[l], b1[l], w2[l], b2[l])
    return (x_pair[0].reshape(1, s_p, d), x_pair[1].reshape(n_smp, s_s, d))
```

```python
import functools
import math

import jax
import jax.numpy as jnp
import numpy as np
from jax import lax
from jax.experimental import pallas as pl
from jax.experimental.pallas import tpu as pltpu

F32 = jnp.float32
BF16 = jnp.bfloat16

D_MODEL = 2048
F_WIDTH = 1024
N_FG = 8
FG_DIM = 128
N_HEADS = 8
HEAD_DIM = 64
V_DIM = 128
QK_WIDTH = 1024
N_EXPERTS = 32
TOP_K = 4
D_FF = 2048
ROPE_THETA = 10000.0
SWIGLU_ALPHA = 1.702
SWIGLU_LIMIT = 7.0
EPS = 1e-5

LANES = 128
HALF_D = D_MODEL // 2
DFT_N2 = 128
VMEM_LIMIT = 48 * 1024 * 1024

MOE_TM = 512
MOE_TF = 512
FILL_SIZES = tuple(MOE_TM >> i for i in range(7))
ATTN_TQ = 512
ATTN_TK = 512
ATTN_UNROLL = 16
SCORE_BOUND_NO_MAX = 40.0


def _cparams(sem):
    return pltpu.CompilerParams(dimension_semantics=sem, vmem_limit_bytes=VMEM_LIMIT)


def _norm_rope_store(acc, g_ref, cos_ref, sin_ref, o_ref, scale):
    tm = acc.shape[0]
    w = 2 * LANES
    r = lax.broadcasted_iota(jnp.int32, (w, w), 0) // HEAD_DIM
    c = lax.broadcasted_iota(jnp.int32, (w, w), 1) // HEAD_DIM
    group_mean = jnp.where(r == c, 1.0 / HEAD_DIM, 0.0).astype(BF16)
    lane = lax.broadcasted_iota(jnp.int32, (tm, w), 1)
    first_half = (lane % HEAD_DIM) < (HEAD_DIM // 2)
    cos = jnp.concatenate([cos_ref[...]] * 2, axis=1)
    sin = jnp.concatenate([sin_ref[...]] * 2, axis=1)
    g = jnp.concatenate([g_ref[...]] * 2, axis=1)
    for h in range(acc.shape[1] // w):
        a = acc[:, h * w:(h + 1) * w]
        sq = a * a
        hi = sq.astype(BF16)
        lo = (sq - hi.astype(F32)).astype(BF16)
        ms = (jnp.dot(hi, group_mean, preferred_element_type=F32)
              + jnp.dot(lo, group_mean, preferred_element_type=F32))
        y = a * lax.rsqrt(ms + EPS) * g
        rot = jnp.where(first_half, -pltpu.roll(y, w - HEAD_DIM // 2, 1),
                        pltpu.roll(y, HEAD_DIM // 2, 1))
        o_ref[:, h * w:(h + 1) * w] = ((y * cos + rot * sin) * scale).astype(BF16)


def _pick(n_first, first_ref, second_ref):
    return jnp.where(pl.program_id(0) < n_first, first_ref[...], second_ref[...])


def _pair_specs(tm, width, n_first, n_grid_axes):
    if n_grid_axes == 1:
        return [pl.BlockSpec((tm, width), lambda i: (jnp.minimum(i, n_first - 1), 0)),
                pl.BlockSpec((tm, width), lambda i: (jnp.maximum(i - n_first, 0), 0))]
    return [pl.BlockSpec((tm, width), lambda i, j: (jnp.minimum(i, n_first - 1), 0)),
            pl.BlockSpec((tm, width), lambda i, j: (jnp.maximum(i - n_first, 0), 0))]


def _inproj_kernel(xp_ref, xs_ref, g1_ref, w_ref, cos_ref, sin_ref, gq_ref, gk_ref,
                   u_ref, q_ref, k_ref, v_ref, h_ref, *, n_p):
    j = pl.program_id(1)

    @pl.when(j == 0)
    def _():
        x = _pick(n_p, xp_ref, xs_ref)
        ms = jnp.mean(x * x, axis=-1, keepdims=True)
        h_ref[...] = (x * lax.rsqrt(ms + EPS) * g1_ref[...]).astype(BF16)

    acc = jnp.dot(h_ref[...], w_ref[...], preferred_element_type=F32)

    @pl.when(j == 0)
    def _():
        u_ref[...] = acc

    @pl.when(j == 1)
    def _():
        _norm_rope_store(acc, gq_ref, cos_ref, sin_ref, q_ref, 1.0 / math.sqrt(HEAD_DIM))

    @pl.when(j == 2)
    def _():
        _norm_rope_store(acc, gk_ref, cos_ref, sin_ref, k_ref, 1.0)

    @pl.when(j == 3)
    def _():
        v_ref[...] = acc.astype(BF16)


def _inproj(xp, xs, g1, w_in_b, cos, sin, gq, gk, *, tm, n_p, n_s):
    T = xp.shape[0] + xs.shape[0]

    def pos_blk(i, j):
        return (jnp.where(i < n_p, i, (i - n_p) % n_s), 0)

    out = jax.ShapeDtypeStruct((T, 1024), BF16)
    out_u = jax.ShapeDtypeStruct((T, 1024), F32)
    return pl.pallas_call(
        functools.partial(_inproj_kernel, n_p=n_p),
        grid=(T // tm, 4),
        in_specs=_pair_specs(tm, D_MODEL, n_p, 2) + [
            pl.BlockSpec((1, D_MODEL), lambda i, j: (0, 0)),
            pl.BlockSpec((D_MODEL, 1024), lambda i, j: (0, j)),
            pl.BlockSpec((tm, LANES), pos_blk),
            pl.BlockSpec((tm, LANES), pos_blk),
            pl.BlockSpec((1, LANES), lambda i, j: (0, 0)),
            pl.BlockSpec((1, LANES), lambda i, j: (0, 0)),
        ],
        out_specs=[pl.BlockSpec((tm, 1024), lambda i, j: (i, 0))] * 4,
        out_shape=[out_u, out, out, out],
        scratch_shapes=[pltpu.VMEM((tm, D_MODEL), BF16)],
        compiler_params=_cparams(("parallel", "arbitrary")),
        name="inproj",
    )(xp, xs, g1, w_in_b, cos, sin, gq, gk)


def _fft1_kernel(x_ref, c1_ref, s1_ref, twc_ref, tws_ref, tr_ref, ti_ref):
    c1 = c1_ref[...]
    s1 = s1_ref[...]
    for t in range(x_ref.shape[1]):
        x = x_ref[:, t, :].astype(BF16)
        a = jnp.dot(c1, x, preferred_element_type=F32)
        b = -jnp.dot(s1, x, preferred_element_type=F32)
        c = jnp.concatenate([twc_ref[t]] * N_FG, axis=1)
        s = jnp.concatenate([tws_ref[t]] * N_FG, axis=1)
        tr_ref[t] = a * c + b * s
        ti_ref[t] = b * c - a * s


def _fft2_kernel(tr_ref, ti_ref, c2x_ref, s2x_ref, cc_ref, sc_ref, wf_ref, bf_ref, o_ref, *, scale):
    n = tr_ref.shape[0] * tr_ref.shape[1]
    tr = tr_ref[...].reshape(n, F_WIDTH).astype(BF16)
    ti = ti_ref[...].reshape(n, F_WIDTH).astype(BF16)
    cc = cc_ref[...]
    sc = sc_ref[...]
    for i in range(tr_ref.shape[1]):
        c2 = c2x_ref[i]
        s2 = s2x_ref[i]
        ur = (jnp.dot(c2, tr, preferred_element_type=F32) + jnp.dot(s2, ti, preferred_element_type=F32))
        ui = (jnp.dot(c2, ti, preferred_element_type=F32) - jnp.dot(s2, tr, preferred_element_type=F32))
        outs = []
        for g in range(N_FG):
            sl = slice(g * FG_DIM, (g + 1) * FG_DIM)
            y = (jnp.dot(ur[:, sl].astype(BF16), cc, preferred_element_type=F32)
                 + jnp.dot(ui[:, sl].astype(BF16), sc, preferred_element_type=F32))
            y = (y * scale).astype(BF16)
            outs.append(jnp.dot(y, wf_ref[g], preferred_element_type=F32) + bf_ref[g])
        o_ref[:, i, :] = jnp.concatenate(outs, axis=1)


@functools.lru_cache(maxsize=None)
def _dft_mats_np(n):
    j = np.arange(n)
    ang = (2.0 * np.pi / n) * ((j[:, None] * j[None, :]) % n)
    return np.cos(ang).astype(np.float32), np.sin(ang).astype(np.float32)


def _dft_mats(n):
    c, s = _dft_mats_np(n)
    return jnp.asarray(c).astype(BF16), jnp.asarray(s).astype(BF16)


@functools.lru_cache(maxsize=None)
def _spread_dft_np(tk1):
    c, s = _dft_mats_np(DFT_N2)
    cx = np.zeros((tk1, DFT_N2, DFT_N2 * tk1), np.float32)
    sx = np.zeros((tk1, DFT_N2, DFT_N2 * tk1), np.float32)
    for i in range(tk1):
        cx[i, :, i::tk1] = c
        sx[i, :, i::tk1] = s
    return cx, sx


@functools.lru_cache(maxsize=None)
def _twiddle_np(n1, seq_len):
    ang = (2.0 * np.pi / seq_len) * (np.arange(DFT_N2)[:, None] * np.arange(n1)[None, :])
    return np.cos(ang).astype(np.float32), np.sin(ang).astype(np.float32)


def _fourier(u3, w_f_b, b_f, *, row0, n_seq, seq_len):
    n1 = seq_len // DFT_N2
    blk0 = row0 // seq_len
    c1, s1 = _dft_mats(n1)
    c2, s2 = _dft_mats(DFT_N2)
    tw_cos, tw_sin = _twiddle_np(n1, seq_len)
    twc = jnp.broadcast_to(jnp.asarray(tw_cos)[:, :, None], (DFT_N2, n1, LANES))
    tws = jnp.broadcast_to(jnp.asarray(tw_sin)[:, :, None], (DFT_N2, n1, LANES))

    tb = 8
    t_shape = jax.ShapeDtypeStruct((n_seq, DFT_N2, n1, F_WIDTH), F32)
    tr, ti = pl.pallas_call(
        _fft1_kernel,
        grid=(n_seq, DFT_N2 // tb),
        in_specs=[
            pl.BlockSpec((n1, tb, F_WIDTH), lambda b, j: (blk0 + b, j, 0)),
            pl.BlockSpec((n1, n1), lambda b, j: (0, 0)),
            pl.BlockSpec((n1, n1), lambda b, j: (0, 0)),
            pl.BlockSpec((tb, n1, LANES), lambda b, j: (j, 0, 0)),
            pl.BlockSpec((tb, n1, LANES), lambda b, j: (j, 0, 0)),
        ],
        out_specs=[pl.BlockSpec((None, tb, n1, F_WIDTH), lambda b, j: (b, j, 0, 0))] * 2,
        out_shape=[t_shape] * 2,
        compiler_params=_cparams(("parallel", "parallel")),
        name="fft_stage1",
    )(u3, c1, s1, twc, tws)

    tk1 = min(8, n1)
    c2x, s2x = (jnp.asarray(m).astype(BF16) for m in _spread_dft_np(tk1))
    scale = 1.0 / math.sqrt(seq_len * FG_DIM)
    const = lambda b, j: (0, 0)
    f = pl.pallas_call(
        functools.partial(_fft2_kernel, scale=scale),
        grid=(n_seq, n1 // tk1),
        in_specs=[
            pl.BlockSpec((None, DFT_N2, tk1, F_WIDTH), lambda b, j: (b, 0, j, 0)),
            pl.BlockSpec((None, DFT_N2, tk1, F_WIDTH), lambda b, j: (b, 0, j, 0)),
            pl.BlockSpec((tk1, DFT_N2, DFT_N2 * tk1), lambda b, j: (0, 0, 0)),
            pl.BlockSpec((tk1, DFT_N2, DFT_N2 * tk1), lambda b, j: (0, 0, 0)),
            pl.BlockSpec((FG_DIM, FG_DIM), const),
            pl.BlockSpec((FG_DIM, FG_DIM), const),
            pl.BlockSpec((N_FG, FG_DIM, FG_DIM), lambda b, j: (0, 0, 0)),
            pl.BlockSpec((N_FG, 1, FG_DIM), lambda b, j: (0, 0, 0)),
        ],
        out_specs=pl.BlockSpec((None, DFT_N2, tk1, F_WIDTH), lambda b, j: (b, 0, j, 0)),
        out_shape=jax.ShapeDtypeStruct((n_seq, DFT_N2, n1, F_WIDTH), F32),
        compiler_params=_cparams(("parallel", "parallel")),
        name="fft_stage2",
    )(tr, ti, c2x, s2x, c2, s2, w_f_b, b_f.reshape(N_FG, 1, FG_DIM))
    return f.reshape(n_seq * seq_len, F_WIDTH)


def _attn_finalize(lam_ref, subg_ref, o_ref, acc0, acc1, l0, l1, lam_init):
    lam_p = lam_ref[...]
    lam = (jnp.exp(jnp.sum(lam_p[0:1] * lam_p[1:2], axis=1, keepdims=True))
           - jnp.exp(jnp.sum(lam_p[2:3] * lam_p[3:4], axis=1, keepdims=True)) + lam_init)
    o = acc0 / l0 - lam * (acc1 / l1)
    ms_o = jnp.mean(o * o, axis=-1, keepdims=True)
    o = o * lax.rsqrt(ms_o + EPS) * subg_ref[...]
    o_ref[...] = (o * (1.0 - lam_init)).astype(BF16)


def _attn_kernel(bounded_ref, lam_ref, subg_ref, q_ref, k_ref, v_ref, o_ref,
                 q2_ref, acc_ref, m_ref, l_ref, ls_ref, *, seq_len, tk, lam_init):
    tq = q_ref.shape[0]
    q = q_ref[...]
    lane = lax.broadcasted_iota(jnp.int32, q.shape, 1)
    zero = jnp.zeros_like(q)
    q2_ref[0:tq, :] = jnp.where(lane < HEAD_DIM, q, zero)
    q2_ref[tq:2 * tq, :] = jnp.where(lane >= HEAD_DIM, q, zero)
    acc_ref[...] = jnp.zeros_like(acc_ref)
    nt = (((1,), (1,)), ((), ()))

    def scores(j):
        start = pl.multiple_of(j * tk, tk)
        s = lax.dot_general(q2_ref[...], k_ref[pl.ds(start, tk), :], nt,
                            preferred_element_type=F32)
        return s, v_ref[pl.ds(start, tk), :]

    def finalize(l):
        acc = acc_ref[...]
        _attn_finalize(lam_ref, subg_ref, o_ref, acc[0:tq], acc[tq:2 * tq],
                       l[0:tq], l[tq:2 * tq], lam_init)

    @pl.when(bounded_ref[0] == 1)
    def _():
        ls_ref[...] = jnp.zeros_like(ls_ref)

        def body(j, carry):
            s, vt = scores(j)
            p = jnp.exp(s)
            part = p[:, 0:LANES]
            for t in range(1, tk // LANES):
                part = part + p[:, t * LANES:(t + 1) * LANES]
            ls_ref[...] += part
            acc_ref[...] += jnp.dot(p.astype(BF16), vt, preferred_element_type=F32)
            return carry

        lax.fori_loop(0, seq_len // tk, body, 0, unroll=ATTN_UNROLL)
        finalize(jnp.sum(ls_ref[...], axis=1, keepdims=True))

    @pl.when(bounded_ref[0] == 0)
    def _():
        m_ref[...] = jnp.full_like(m_ref, -jnp.inf)
        l_ref[...] = jnp.zeros_like(l_ref)

        def body(j, carry):
            s, vt = scores(j)
            m_prev = m_ref[...]
            m_new = jnp.maximum(m_prev, jnp.max(s, axis=1, keepdims=True))
            alpha = jnp.exp(m_prev - m_new)
            p = jnp.exp(s - m_new)
            l_ref[...] = alpha * l_ref[...] + jnp.sum(p, axis=1, keepdims=True)
            acc_ref[...] = alpha * acc_ref[...] + jnp.dot(p.astype(BF16), vt,
                                                          preferred_element_type=F32)
            m_ref[...] = m_new
            return carry

        lax.fori_loop(0, seq_len // tk, body, 0)
        finalize(l_ref[...])


def _attention(q, k, v, bounded, lam_p, sub_g, *, row0, n_seq, seq_len, lam_init, tq, tk):
    nq = seq_len // tq
    q0 = row0 // tq
    kv0 = row0 // seq_len
    kernel = functools.partial(_attn_kernel, seq_len=seq_len, tk=tk, lam_init=lam_init)
    grid_spec = pltpu.PrefetchScalarGridSpec(
        num_scalar_prefetch=1,
        grid=(n_seq, N_HEADS, nq),
        in_specs=[
            pl.BlockSpec((4, HEAD_DIM), lambda b, h, i, f: (0, 0)),
            pl.BlockSpec((1, V_DIM), lambda b, h, i, f: (0, 0)),
            pl.BlockSpec((tq, LANES), lambda b, h, i, f: (q0 + b * nq + i, h)),
            pl.BlockSpec((seq_len, LANES), lambda b, h, i, f: (kv0 + b, h)),
            pl.BlockSpec((seq_len, LANES), lambda b, h, i, f: (kv0 + b, h)),
        ],
        out_specs=pl.BlockSpec((tq, V_DIM), lambda b, h, i, f: (b * nq + i, h)),
        scratch_shapes=[pltpu.VMEM((2 * tq, LANES), BF16), pltpu.VMEM((2 * tq, V_DIM), F32),
                        pltpu.VMEM((2 * tq, 1), F32), pltpu.VMEM((2 * tq, 1), F32),
                        pltpu.VMEM((2 * tq, LANES), F32)],
    )
    return pl.pallas_call(
        kernel,
        grid_spec=grid_spec,
        out_shape=jax.ShapeDtypeStruct((n_seq * seq_len, N_HEADS * V_DIM), BF16),
        compiler_params=_cparams(("parallel", "parallel", "arbitrary")),
        name="diff_attn",
    )(bounded, lam_p, sub_g, q, k, v)


def _outproj_kernel(fp_ref, fs_ref, ap_ref, as_ref, xp_ref, xs_ref, wo_ref, g2_ref, wr_ref, br_ref,
                    x1_ref, h2_ref, idx_ref, gexp_ref, *, n_p):
    acc = (jnp.dot(_pick(n_p, fp_ref, fs_ref).astype(BF16), wo_ref[0:F_WIDTH, :],
                   preferred_element_type=F32)
           + jnp.dot(_pick(n_p, ap_ref, as_ref), wo_ref[F_WIDTH:D_MODEL, :],
                     preferred_element_type=F32))
    x1 = _pick(n_p, xp_ref, xs_ref) + acc
    x1_ref[...] = x1
    ms = jnp.mean(x1 * x1, axis=-1, keepdims=True)
    h2 = (x1 * lax.rsqrt(ms + EPS) * g2_ref[...]).astype(BF16)
    lo = lax.bitcast_convert_type(h2[:, 0:HALF_D].astype(F32), jnp.uint32)
    hi = lax.bitcast_convert_type(h2[:, HALF_D:D_MODEL].astype(F32), jnp.uint32)
    h2_ref[...] = lax.shift_right_logical(lo, jnp.uint32(16)) | (hi & jnp.uint32(0xFFFF0000))
    logits = jnp.dot(h2, wr_ref[...], preferred_element_type=F32) + br_ref[...]
    tm = logits.shape[0]
    lane = lax.broadcasted_iota(jnp.int32, (tm, N_EXPERTS), 1)
    cur = logits
    vals, idxs = [], []
    for _ in range(TOP_K):
        m = jnp.max(cur, axis=1, keepdims=True)
        am = jnp.min(jnp.where(cur == m, lane, N_EXPERTS), axis=1, keepdims=True)
        vals.append(m)
        idxs.append(am)
        cur = jnp.where(lane == am, -jnp.inf, cur)
    idx_ref[...] = jnp.concatenate(idxs, axis=1)
    es = [jnp.exp(v - vals[0]) for v in vals]
    den = es[0] + es[1] + es[2] + es[3]
    gexp_ref[...] = jnp.concatenate([jnp.broadcast_to(e / den, (tm, LANES)) for e in es], axis=1)


def _outproj(f_pair, a_pair, x_pair, w_out_b, g2, w_r_b, b_r, *, tm):
    T = x_pair[0].shape[0] + x_pair[1].shape[0]
    n_p = x_pair[0].shape[0] // tm
    const = lambda i: (0, 0)
    return pl.pallas_call(
        functools.partial(_outproj_kernel, n_p=n_p),
        grid=(T // tm,),
        in_specs=_pair_specs(tm, F_WIDTH, n_p, 1) + _pair_specs(tm, F_WIDTH, n_p, 1)
        + _pair_specs(tm, D_MODEL, n_p, 1) + [
            pl.BlockSpec((D_MODEL, D_MODEL), const),
            pl.BlockSpec((1, D_MODEL), const),
            pl.BlockSpec((D_MODEL, N_EXPERTS), const),
            pl.BlockSpec((1, N_EXPERTS), const),
        ],
        out_specs=[
            pl.BlockSpec((tm, D_MODEL), lambda i: (i, 0)),
            pl.BlockSpec((tm, HALF_D), lambda i: (i, 0)),
            pl.BlockSpec((tm, TOP_K), lambda i: (i, 0)),
            pl.BlockSpec((tm, TOP_K * LANES), lambda i: (i, 0)),
        ],
        out_shape=[
            jax.ShapeDtypeStruct((T, D_MODEL), F32),
            jax.ShapeDtypeStruct((T, HALF_D), jnp.uint32),
            jax.ShapeDtypeStruct((T, TOP_K), jnp.int32),
            jax.ShapeDtypeStruct((T, TOP_K * LANES), F32),
        ],
        compiler_params=_cparams(("parallel",)),
        name="outproj_router",
    )(*f_pair, *a_pair, *x_pair, w_out_b, g2, w_r_b, b_r)


def _route(top_idx, n_blk):
    flat_e = top_idx.reshape(-1)
    onehot = (flat_e[:, None] == jnp.arange(N_EXPERTS, dtype=jnp.int32)[None, :]).astype(jnp.int32)
    csum = jnp.cumsum(onehot, axis=0)
    counts = csum[-1]
    rank = jnp.sum(csum * onehot, axis=1) - 1
    nblk_e = (counts + MOE_TM - 1) // MOE_TM
    blk_end = jnp.cumsum(nblk_e)
    blk_start = blk_end - nblk_e
    dest = jnp.sum(onehot * (blk_start * MOE_TM)[None, :], axis=1) + rank
    b = jnp.arange(n_blk, dtype=jnp.int32)
    n_used = blk_end[-1]
    blk_src = jnp.minimum(b, n_used - 1)
    blk_e = jnp.minimum(jnp.sum((blk_end[None, :] <= blk_src[:, None]).astype(jnp.int32), axis=1),
                        N_EXPERTS - 1)
    blk_used = (b < n_used).astype(jnp.int32)
    pad_start = (blk_start * MOE_TM + counts) // 8 * 8
    pad_len = blk_end * MOE_TM - pad_start
    fill = jnp.concatenate([pad_start, pad_len, n_used[None]]).astype(jnp.int32)
    return dest.astype(jnp.int32), blk_e.astype(jnp.int32), blk_src.astype(jnp.int32), blk_used, fill


def _w1_split_kernel(w_ref, perm_ref, g_ref, l_ref):
    perm = perm_ref[...]
    half = perm.shape[0] // 2
    for c in range(w_ref.shape[2] // perm.shape[0]):
        w = w_ref[0, :, c * 2 * half:(c + 1) * 2 * half].astype(BF16)
        r = jnp.dot(w, perm, preferred_element_type=F32)
        g_ref[0, :, c * half:(c + 1) * half] = r[:, :half].astype(BF16)
        l_ref[0, :, c * half:(c + 1) * half] = r[:, half:].astype(BF16)


def _w1_split(w1):
    n_e, d, f2 = w1.shape
    chunk = 2 * LANES
    tcol = 512
    i = jnp.arange(chunk, dtype=jnp.int32)
    src = jnp.where(i < LANES, 2 * i, 2 * (i - LANES) + 1)
    perm = (i[:, None] == src[None, :]).astype(BF16)
    out = jax.ShapeDtypeStruct((n_e, d, f2 // 2), BF16)
    return pl.pallas_call(
        _w1_split_kernel,
        grid=(n_e, f2 // tcol),
        in_specs=[pl.BlockSpec((1, d, tcol), lambda e, j: (e, 0, j)),
                  pl.BlockSpec((chunk, chunk), lambda e, j: (0, 0))],
        out_specs=[pl.BlockSpec((1, d, tcol // 2), lambda e, j: (e, 0, j))] * 2,
        out_shape=[out, out],
        compiler_params=_cparams(("parallel", "parallel")),
        name="w1_split",
    )(w1, perm)


def _scatter_kernel(fill_ref, dest_ref, h_ref, xs_ref, idx_smem, zero_ref, idx_sem, sem, zero_sem,
                    *, te, n_blk):
    @pl.when(pl.program_id(0) == 0)
    def _():
        zero_ref[...] = jnp.zeros_like(zero_ref)

        def fill_copy(row, size):
            return pltpu.make_async_copy(zero_ref.at[pl.ds(0, size), :],
                                         xs_ref.at[pl.ds(row, size), :], zero_sem)

        def expert_fills(act):
            for e in range(N_EXPERTS):
                row = fill_ref[e]
                for size in FILL_SIZES:
                    take = (fill_ref[N_EXPERTS + e] & size) != 0

                    @pl.when(take)
                    def _(row=row, size=size):
                        act(fill_copy(pl.multiple_of(row, 8), size))

                    row = row + jnp.where(take, size, 0)

        def block_fills(act):
            def one(b, carry):
                act(fill_copy(pl.multiple_of(b * MOE_TM, MOE_TM), MOE_TM))
                return carry

            lax.fori_loop(fill_ref[2 * N_EXPERTS], n_blk, one, 0)

        expert_fills(lambda cp: cp.start())
        block_fills(lambda cp: cp.start())
        expert_fills(lambda cp: cp.wait())
        block_fills(lambda cp: cp.wait())

    cp = pltpu.make_async_copy(dest_ref.at[0, 0], idx_smem, idx_sem)
    cp.start()
    cp.wait()

    def row_copy(t, d):
        return pltpu.make_async_copy(h_ref.at[pl.ds(t, 1), :], xs_ref.at[pl.ds(d, 1), :], sem)

    def issue(t, carry):
        for k in range(TOP_K):
            row_copy(t, idx_smem[TOP_K * t + k]).start()
        return carry

    lax.fori_loop(0, te, issue, 0)
    for k in range(TOP_K):
        pltpu.make_async_copy(h_ref, xs_ref.at[pl.ds(0, te), :], sem).wait()


def _moe_scatter(h2p, dest, fill, n_rows, *, te):
    T = h2p.shape[0]
    dest3 = dest.reshape(T // te, 1, te * TOP_K)
    n_blk = n_rows // MOE_TM
    grid_spec = pltpu.PrefetchScalarGridSpec(
        num_scalar_prefetch=1,
        grid=(T // te,),
        in_specs=[
            pl.BlockSpec((1, 1, te * TOP_K), lambda i, fl: (i, 0, 0)),
            pl.BlockSpec((te, HALF_D), lambda i, fl: (i, 0)),
        ],
        out_specs=pl.BlockSpec(memory_space=pl.ANY),
        scratch_shapes=[pltpu.SMEM((te * TOP_K,), jnp.int32),
                        pltpu.VMEM((MOE_TM, HALF_D), jnp.uint32),
                        pltpu.SemaphoreType.DMA(()), pltpu.SemaphoreType.DMA(()),
                        pltpu.SemaphoreType.DMA(())],
    )
    return pl.pallas_call(
        functools.partial(_scatter_kernel, te=te, n_blk=n_blk),
        grid_spec=grid_spec,
        out_shape=jax.ShapeDtypeStruct((n_rows, HALF_D), jnp.uint32),
        compiler_params=_cparams(("arbitrary",)),
        name="moe_scatter",
    )(fill, dest3, h2p)


def _moe_up_kernel(be_ref, bsrc_ref, bused_ref, x_ref, w1g_ref, w1l_ref, b1g_ref, b1l_ref,
                   o_ref, xb_ref):
    b = pl.program_id(0)

    @pl.when(bused_ref[b] > 0)
    def _():
        w = x_ref[...]
        lo = lax.bitcast_convert_type(lax.shift_left(w, jnp.uint32(16)), F32)
        hi = lax.bitcast_convert_type(w & jnp.uint32(0xFFFF0000), F32)
        xb_ref[:, 0:HALF_D] = lo.astype(BF16)
        xb_ref[:, HALF_D:D_MODEL] = hi.astype(BF16)
        x = xb_ref[...]
        for c in range(D_FF // MOE_TF):
            sl = slice(c * MOE_TF, (c + 1) * MOE_TF)
            hg = jnp.dot(x, w1g_ref[0, :, sl], preferred_element_type=F32) + b1g_ref[0, :, sl]
            hl = jnp.dot(x, w1l_ref[0, :, sl], preferred_element_type=F32) + b1l_ref[0, :, sl]
            hg = jnp.minimum(hg, SWIGLU_LIMIT)
            hl = jnp.clip(hl, -SWIGLU_LIMIT, SWIGLU_LIMIT)
            o_ref[:, sl] = (hg * jax.nn.sigmoid(SWIGLU_ALPHA * hg) * (hl + 1.0)).astype(BF16)

    @pl.when(bused_ref[b] == 0)
    def _():
        o_ref[...] = jnp.zeros_like(o_ref)


def _moe_down_kernel(be_ref, bsrc_ref, bused_ref, a_ref, w2_ref, b2_ref, o_ref):
    b = pl.program_id(0)

    @pl.when(bused_ref[b] > 0)
    def _():
        o_ref[...] = jnp.dot(a_ref[...], w2_ref[0], preferred_element_type=F32) + b2_ref[0]

    @pl.when(bused_ref[b] == 0)
    def _():
        o_ref[...] = jnp.zeros_like(o_ref)


def _moe_ffn(xs, blk_e, blk_src, blk_used, w1g, w1l, b1g, b1l, w2b, b2):
    n_rows = xs.shape[0]
    n_blk = n_rows // MOE_TM
    up_spec = pltpu.PrefetchScalarGridSpec(
        num_scalar_prefetch=3,
        grid=(n_blk,),
        in_specs=[
            pl.BlockSpec((MOE_TM, HALF_D), lambda b, be, bs, bu: (bs[b], 0)),
            pl.BlockSpec((1, D_MODEL, D_FF), lambda b, be, bs, bu: (be[b], 0, 0)),
            pl.BlockSpec((1, D_MODEL, D_FF), lambda b, be, bs, bu: (be[b], 0, 0)),
            pl.BlockSpec((1, 1, D_FF), lambda b, be, bs, bu: (be[b], 0, 0)),
            pl.BlockSpec((1, 1, D_FF), lambda b, be, bs, bu: (be[b], 0, 0)),
        ],
        out_specs=pl.BlockSpec((MOE_TM, D_FF), lambda b, be, bs, bu: (b, 0)),
        scratch_shapes=[pltpu.VMEM((MOE_TM, D_MODEL), BF16)],
    )
    act = pl.pallas_call(
        _moe_up_kernel,
        grid_spec=up_spec,
        out_shape=jax.ShapeDtypeStruct((n_rows, D_FF), BF16),
        compiler_params=_cparams(("arbitrary",)),
        name="moe_up",
    )(blk_e, blk_src, blk_used, xs, w1g, w1l, b1g, b1l)
    down_spec = pltpu.PrefetchScalarGridSpec(
        num_scalar_prefetch=3,
        grid=(n_blk,),
        in_specs=[
            pl.BlockSpec((MOE_TM, D_FF), lambda b, be, bs, bu: (bs[b], 0)),
            pl.BlockSpec((1, D_FF, D_MODEL), lambda b, be, bs, bu: (be[b], 0, 0)),
            pl.BlockSpec((1, 1, D_MODEL), lambda b, be, bs, bu: (be[b], 0, 0)),
        ],
        out_specs=pl.BlockSpec((MOE_TM, D_MODEL), lambda b, be, bs, bu: (b, 0)),
    )
    return pl.pallas_call(
        _moe_down_kernel,
        grid_spec=down_spec,
        out_shape=jax.ShapeDtypeStruct((n_rows, D_MODEL), F32),
        compiler_params=_cparams(("arbitrary",)),
        name="moe_down",
    )(blk_e, blk_src, blk_used, act, w2b, b2)


def _combine_kernel(dest_ref, x1_ref, gexp_ref, ys_ref, o_ref, idx_smem, gbuf, idx_sem, sem, *, tc):
    cp = pltpu.make_async_copy(dest_ref.at[0, 0], idx_smem, idx_sem)
    cp.start()
    cp.wait()

    def row_copy(d, r):
        return pltpu.make_async_copy(ys_ref.at[pl.ds(d, 1), :], gbuf.at[pl.ds(r, 1), :], sem)

    def issue(t, carry):
        for k in range(TOP_K):
            row_copy(idx_smem[TOP_K * t + k], k * tc + t).start()
        return carry

    lax.fori_loop(0, tc, issue, 0)
    pltpu.make_async_copy(ys_ref.at[pl.ds(0, TOP_K * tc), :], gbuf, sem).wait()

    acc = x1_ref[...]
    for k in range(TOP_K):
        gate = jnp.concatenate([gexp_ref[:, k * LANES:(k + 1) * LANES]] * (D_MODEL // LANES), axis=1)
        acc = acc + gate * gbuf[pl.ds(k * tc, tc), :]
    o_ref[...] = acc


def _moe_combine(x1, gexp, dest, ys, *, row0, n_rows, tc):
    T = x1.shape[0]
    i0 = row0 // tc
    return pl.pallas_call(
        functools.partial(_combine_kernel, tc=tc),
        grid=(n_rows // tc,),
        in_specs=[
            pl.BlockSpec((1, 1, tc * TOP_K), lambda i: (i0 + i, 0, 0)),
            pl.BlockSpec((tc, D_MODEL), lambda i: (i0 + i, 0)),
            pl.BlockSpec((tc, TOP_K * LANES), lambda i: (i0 + i, 0)),
            pl.BlockSpec(memory_space=pl.ANY),
        ],
        out_specs=pl.BlockSpec((tc, D_MODEL), lambda i: (i, 0)),
        out_shape=jax.ShapeDtypeStruct((n_rows, D_MODEL), F32),
        scratch_shapes=[pltpu.SMEM((tc * TOP_K,), jnp.int32),
                        pltpu.VMEM((tc * TOP_K, D_MODEL), F32),
                        pltpu.SemaphoreType.DMA(()), pltpu.SemaphoreType.DMA(())],
        compiler_params=_cparams(("arbitrary",)),
        name="moe_combine",
    )(dest.reshape(T // tc, 1, tc * TOP_K), x1, gexp, ys)


@functools.lru_cache(maxsize=None)
def _rope_tables_np(n_pos):
    inv = (np.float32(1.0) / np.power(np.float32(ROPE_THETA),
                                      np.arange(0, HEAD_DIM, 2, dtype=np.float32) / np.float32(HEAD_DIM)))
    ang = np.arange(n_pos, dtype=np.float32)[:, None] * inv.astype(np.float32)[None, :]
    ang = ang.astype(np.float32).astype(np.float64)
    return np.cos(ang).astype(np.float32), np.sin(ang).astype(np.float32)


def _rope_tables(n_pos):
    c, s = _rope_tables_np(n_pos)
    return jnp.tile(jnp.asarray(c), (1, 4)), jnp.tile(jnp.asarray(s), (1, 4))


def _layer(x_pair, l, s_p, n_smp, s_s, norm1_g, w_in, q_norm_g, k_norm_g, lambda_q1, lambda_k1,
           lambda_q2, lambda_k2, sub_norm_g, w_fourier, b_fourier, w_out, norm2_g,
           w_router, b_router, w1, b1, w2, b2):
    T = x_pair[0].shape[0] + x_pair[1].shape[0]
    lam_init = 0.8 - 0.6 * math.exp(-0.3 * l)
    tm = min(512, s_s)

    cos, sin = _rope_tables(max(s_p, s_s))
    u, q, k, v = _inproj(
        *x_pair, norm1_g.reshape(1, D_MODEL), w_in.astype(BF16), cos, sin,
        jnp.tile(q_norm_g, 2).reshape(1, LANES), jnp.tile(k_norm_g, 2).reshape(1, LANES),
        tm=tm, n_p=s_p // tm, n_s=s_s // tm)

    w_f_b = w_fourier.astype(BF16)
    u3 = u.reshape(T // DFT_N2, DFT_N2, F_WIDTH)
    f_pair = (_fourier(u3, w_f_b, b_fourier, row0=0, n_seq=1, seq_len=s_p),
              _fourier(u3, w_f_b, b_fourier, row0=s_p, n_seq=n_smp, seq_len=s_s))

    lam_p = jnp.stack([lambda_q1, lambda_k1, lambda_q2, lambda_k2]).astype(F32)
    sub_g = sub_norm_g.reshape(1, V_DIM)
    score_bound = (1.01 * math.sqrt(HEAD_DIM)) * jnp.max(jnp.abs(q_norm_g)) * jnp.max(jnp.abs(k_norm_g))
    bounded = (score_bound <= SCORE_BOUND_NO_MAX).astype(jnp.int32).reshape(1)
    a_pair = (_attention(q, k, v, bounded, lam_p, sub_g, row0=0, n_seq=1, seq_len=s_p,
                         lam_init=lam_init, tq=min(ATTN_TQ, s_p), tk=min(ATTN_TK, s_p)),
              _attention(q, k, v, bounded, lam_p, sub_g, row0=s_p, n_seq=n_smp, seq_len=s_s,
                         lam_init=lam_init, tq=min(ATTN_TQ, s_s), tk=min(ATTN_TK, s_s)))

    x1, h2, top_idx, gexp = _outproj(
        f_pair, a_pair, x_pair, w_out.astype(BF16), norm2_g.reshape(1, D_MODEL), w_router.astype(BF16),
        b_router.reshape(1, N_EXPERTS), tm=min(256, s_s))

    n_blk = T * TOP_K // MOE_TM + N_EXPERTS
    dest, blk_e, blk_src, blk_used, fill = _route(top_idx, n_blk)
    xs = _moe_scatter(h2, dest, fill, n_blk * MOE_TM, te=min(256, s_s))
    w1g, w1l = _w1_split(w1)
    ys = _moe_ffn(
        xs, blk_e, blk_src, blk_used, w1g, w1l,
        b1[:, 0::2].reshape(N_EXPERTS, 1, D_FF), b1[:, 1::2].reshape(N_EXPERTS, 1, D_FF),
        w2.astype(BF16), b2.reshape(N_EXPERTS, 1, D_MODEL))
    tc = min(128, s_s)
    return (_moe_combine(x1, gexp, dest, ys, row0=0, n_rows=s_p, tc=tc),
            _moe_combine(x1, gexp, dest, ys, row0=s_p, n_rows=T - s_p, tc=tc))


def kernel(x_prompt, x_sample, norm1_g, w_in, q_norm_g, k_norm_g, lambda_q1, lambda_k1, lambda_q2,
           lambda_k2, sub_norm_g, w_fourier, b_fourier, w_out, norm2_g, w_router, b_router,
           w1, b1, w2, b2):
    b_p, s_p, d = x_prompt.shape
    n_smp, s_s, _ = x_sample.shape
    assert b_p == 1 and d == D_MODEL
    x_pair = (x_prompt.reshape(s_p, d), x_sample.reshape(n_smp * s_s, d))
    for l in range(norm1_g.shape[0]):
        x_pair = _layer(x_pair, l, s_p, n_smp, s_s, norm1_g[l], w_in[l], q_norm_g[l], k_norm_g[l],
                        lambda_q1[l], lambda_k1[l], lambda_q2[l], lambda_k2[l], sub_norm_g[l],
                        w_fourier[l], b_fourier[l], w_out[l], norm2_g[l], w_router[l], b_router[l],
                        w1[l], b1[l], w2[l], b2[l])
    return (x_pair[0].reshape(1, s_p, d), x_pair[1].reshape(n_smp, s_s, d))
```

```python
import functools
import math

import jax
import jax.numpy as jnp
import numpy as np
from jax import lax
from jax.experimental import pallas as pl
from jax.experimental.pallas import tpu as pltpu

F32 = jnp.float32
BF16 = jnp.bfloat16

D_MODEL = 2048
F_WIDTH = 1024
N_FG = 8
FG_DIM = 128
N_HEADS = 8
HEAD_DIM = 64
V_DIM = 128
QK_WIDTH = 1024
N_EXPERTS = 32
TOP_K = 4
D_FF = 2048
ROPE_THETA = 10000.0
SWIGLU_ALPHA = 1.702
SWIGLU_LIMIT = 7.0
EPS = 1e-5

LANES = 128
HALF_D = D_MODEL // 2
DFT_N2 = 128
VMEM_LIMIT = 48 * 1024 * 1024

MOE_TM = 512
MOE_TF = 512
FILL_SIZES = tuple(MOE_TM >> i for i in range(7))
ATTN_TQ = 512
ATTN_TK = 512
ATTN_UNROLL = 16
SCORE_BOUND_NO_MAX = 40.0


def _cparams(sem):
    return pltpu.CompilerParams(dimension_semantics=sem, vmem_limit_bytes=VMEM_LIMIT)


def _norm_rope_store(acc, g_ref, cos_ref, sin_ref, o_ref, scale):
    tm = acc.shape[0]
    w = 2 * LANES
    r = lax.broadcasted_iota(jnp.int32, (w, w), 0) // HEAD_DIM
    c = lax.broadcasted_iota(jnp.int32, (w, w), 1) // HEAD_DIM
    group_mean = jnp.where(r == c, 1.0 / HEAD_DIM, 0.0).astype(BF16)
    lane = lax.broadcasted_iota(jnp.int32, (tm, w), 1)
    first_half = (lane % HEAD_DIM) < (HEAD_DIM // 2)
    cos = jnp.concatenate([cos_ref[...]] * 2, axis=1)
    sin = jnp.concatenate([sin_ref[...]] * 2, axis=1)
    g = jnp.concatenate([g_ref[...]] * 2, axis=1)
    for h in range(acc.shape[1] // w):
        a = acc[:, h * w:(h + 1) * w]
        sq = a * a
        hi = sq.astype(BF16)
        lo = (sq - hi.astype(F32)).astype(BF16)
        ms = (jnp.dot(hi, group_mean, preferred_element_type=F32)
              + jnp.dot(lo, group_mean, preferred_element_type=F32))
        y = a * lax.rsqrt(ms + EPS) * g
        rot = jnp.where(first_half, -pltpu.roll(y, w - HEAD_DIM // 2, 1),
                        pltpu.roll(y, HEAD_DIM // 2, 1))
        o_ref[:, h * w:(h + 1) * w] = ((y * cos + rot * sin) * scale).astype(BF16)


def _pick(n_first, first_ref, second_ref):
    return jnp.where(pl.program_id(0) < n_first, first_ref[...], second_ref[...])


def _pair_specs(tm, width, n_first):
    return [pl.BlockSpec((tm, width), lambda i: (jnp.minimum(i, n_first - 1), 0)),
            pl.BlockSpec((tm, width), lambda i: (jnp.maximum(i - n_first, 0), 0))]


def _inproj_kernel(xp_ref, xs_ref, g1_ref, w_ref, cos_ref, sin_ref, gq_ref, gk_ref,
                   u_ref, q_ref, k_ref, v_ref, *, n_p):
    x = _pick(n_p, xp_ref, xs_ref)
    ms = jnp.mean(x * x, axis=-1, keepdims=True)
    h = (x * lax.rsqrt(ms + EPS) * g1_ref[...]).astype(BF16)

    def proj(j):
        return jnp.dot(h, w_ref[:, j * QK_WIDTH:(j + 1) * QK_WIDTH], preferred_element_type=F32)

    u_ref[...] = proj(0)
    _norm_rope_store(proj(1), gq_ref, cos_ref, sin_ref, q_ref, 1.0 / math.sqrt(HEAD_DIM))
    _norm_rope_store(proj(2), gk_ref, cos_ref, sin_ref, k_ref, 1.0)
    v_ref[...] = proj(3).astype(BF16)


def _inproj(xp, xs, g1, w_in_b, cos, sin, gq, gk, *, tm, n_p, n_s):
    T = xp.shape[0] + xs.shape[0]

    def pos_blk(i):
        return (jnp.where(i < n_p, i, (i - n_p) % n_s), 0)

    const = lambda i: (0, 0)
    out = jax.ShapeDtypeStruct((T, QK_WIDTH), BF16)
    out_u = jax.ShapeDtypeStruct((T, F_WIDTH), F32)
    return pl.pallas_call(
        functools.partial(_inproj_kernel, n_p=n_p),
        grid=(T // tm,),
        in_specs=_pair_specs(tm, D_MODEL, n_p) + [
            pl.BlockSpec((1, D_MODEL), const),
            pl.BlockSpec((D_MODEL, 4 * QK_WIDTH), const),
            pl.BlockSpec((tm, LANES), pos_blk),
            pl.BlockSpec((tm, LANES), pos_blk),
            pl.BlockSpec((1, LANES), const),
            pl.BlockSpec((1, LANES), const),
        ],
        out_specs=[pl.BlockSpec((tm, QK_WIDTH), lambda i: (i, 0))] * 4,
        out_shape=[out_u, out, out, out],
        compiler_params=_cparams(("parallel",)),
        name="inproj",
    )(xp, xs, g1, w_in_b, cos, sin, gq, gk)


def _fft1_kernel(x_ref, c1_ref, s1_ref, twc_ref, tws_ref, tr_ref, ti_ref):
    c1 = c1_ref[...]
    s1 = s1_ref[...]
    for t in range(x_ref.shape[1]):
        x = x_ref[:, t, :].astype(BF16)
        a = jnp.dot(c1, x, preferred_element_type=F32)
        b = -jnp.dot(s1, x, preferred_element_type=F32)
        c = jnp.concatenate([twc_ref[t]] * N_FG, axis=1)
        s = jnp.concatenate([tws_ref[t]] * N_FG, axis=1)
        tr_ref[t] = a * c + b * s
        ti_ref[t] = b * c - a * s


def _fft2_kernel(tr_ref, ti_ref, c2x_ref, s2x_ref, cc_ref, sc_ref, wf_ref, bf_ref, o_ref, *, scale):
    n = tr_ref.shape[0] * tr_ref.shape[1]
    tr = tr_ref[...].reshape(n, F_WIDTH).astype(BF16)
    ti = ti_ref[...].reshape(n, F_WIDTH).astype(BF16)
    cc = cc_ref[...]
    sc = sc_ref[...]
    for i in range(tr_ref.shape[1]):
        c2 = c2x_ref[i]
        s2 = s2x_ref[i]
        ur = (jnp.dot(c2, tr, preferred_element_type=F32) + jnp.dot(s2, ti, preferred_element_type=F32))
        ui = (jnp.dot(c2, ti, preferred_element_type=F32) - jnp.dot(s2, tr, preferred_element_type=F32))
        outs = []
        for g in range(N_FG):
            sl = slice(g * FG_DIM, (g + 1) * FG_DIM)
            y = (jnp.dot(ur[:, sl].astype(BF16), cc, preferred_element_type=F32)
                 + jnp.dot(ui[:, sl].astype(BF16), sc, preferred_element_type=F32))
            y = (y * scale).astype(BF16)
            outs.append(jnp.dot(y, wf_ref[g], preferred_element_type=F32) + bf_ref[g])
        o_ref[:, i, :] = jnp.concatenate(outs, axis=1)


@functools.lru_cache(maxsize=None)
def _dft_mats_np(n):
    j = np.arange(n)
    ang = (2.0 * np.pi / n) * ((j[:, None] * j[None, :]) % n)
    return np.cos(ang).astype(np.float32), np.sin(ang).astype(np.float32)


def _dft_mats(n):
    c, s = _dft_mats_np(n)
    return jnp.asarray(c).astype(BF16), jnp.asarray(s).astype(BF16)


@functools.lru_cache(maxsize=None)
def _spread_dft_np(tk1):
    c, s = _dft_mats_np(DFT_N2)
    cx = np.zeros((tk1, DFT_N2, DFT_N2 * tk1), np.float32)
    sx = np.zeros((tk1, DFT_N2, DFT_N2 * tk1), np.float32)
    for i in range(tk1):
        cx[i, :, i::tk1] = c
        sx[i, :, i::tk1] = s
    return cx, sx


@functools.lru_cache(maxsize=None)
def _twiddle_np(n1, seq_len):
    ang = (2.0 * np.pi / seq_len) * (np.arange(DFT_N2)[:, None] * np.arange(n1)[None, :])
    return np.cos(ang).astype(np.float32), np.sin(ang).astype(np.float32)


def _fourier(u3, w_f_b, b_f, *, row0, n_seq, seq_len):
    n1 = seq_len // DFT_N2
    blk0 = row0 // seq_len
    c1, s1 = _dft_mats(n1)
    c2, s2 = _dft_mats(DFT_N2)
    tw_cos, tw_sin = _twiddle_np(n1, seq_len)
    twc = jnp.broadcast_to(jnp.asarray(tw_cos)[:, :, None], (DFT_N2, n1, LANES))
    tws = jnp.broadcast_to(jnp.asarray(tw_sin)[:, :, None], (DFT_N2, n1, LANES))

    tb = 8
    t_shape = jax.ShapeDtypeStruct((n_seq, DFT_N2, n1, F_WIDTH), F32)
    tr, ti = pl.pallas_call(
        _fft1_kernel,
        grid=(n_seq, DFT_N2 // tb),
        in_specs=[
            pl.BlockSpec((n1, tb, F_WIDTH), lambda b, j: (blk0 + b, j, 0)),
            pl.BlockSpec((n1, n1), lambda b, j: (0, 0)),
            pl.BlockSpec((n1, n1), lambda b, j: (0, 0)),
            pl.BlockSpec((tb, n1, LANES), lambda b, j: (j, 0, 0)),
            pl.BlockSpec((tb, n1, LANES), lambda b, j: (j, 0, 0)),
        ],
        out_specs=[pl.BlockSpec((None, tb, n1, F_WIDTH), lambda b, j: (b, j, 0, 0))] * 2,
        out_shape=[t_shape] * 2,
        compiler_params=_cparams(("parallel", "parallel")),
        name="fft_stage1",
    )(u3, c1, s1, twc, tws)

    tk1 = min(8, n1)
    c2x, s2x = (jnp.asarray(m).astype(BF16) for m in _spread_dft_np(tk1))
    scale = 1.0 / math.sqrt(seq_len * FG_DIM)
    const = lambda b, j: (0, 0)
    f = pl.pallas_call(
        functools.partial(_fft2_kernel, scale=scale),
        grid=(n_seq, n1 // tk1),
        in_specs=[
            pl.BlockSpec((None, DFT_N2, tk1, F_WIDTH), lambda b, j: (b, 0, j, 0)),
            pl.BlockSpec((None, DFT_N2, tk1, F_WIDTH), lambda b, j: (b, 0, j, 0)),
            pl.BlockSpec((tk1, DFT_N2, DFT_N2 * tk1), lambda b, j: (0, 0, 0)),
            pl.BlockSpec((tk1, DFT_N2, DFT_N2 * tk1), lambda b, j: (0, 0, 0)),
            pl.BlockSpec((FG_DIM, FG_DIM), const),
            pl.BlockSpec((FG_DIM, FG_DIM), const),
            pl.BlockSpec((N_FG, FG_DIM, FG_DIM), lambda b, j: (0, 0, 0)),
            pl.BlockSpec((N_FG, 1, FG_DIM), lambda b, j: (0, 0, 0)),
        ],
        out_specs=pl.BlockSpec((None, DFT_N2, tk1, F_WIDTH), lambda b, j: (b, 0, j, 0)),
        out_shape=jax.ShapeDtypeStruct((n_seq, DFT_N2, n1, F_WIDTH), F32),
        compiler_params=_cparams(("parallel", "parallel")),
        name="fft_stage2",
    )(tr, ti, c2x, s2x, c2, s2, w_f_b, b_f.reshape(N_FG, 1, FG_DIM))
    return f.reshape(n_seq * seq_len, F_WIDTH)


def _attn_finalize(lam_ref, subg_ref, o_ref, acc0, acc1, l0, l1, lam_init):
    lam_p = lam_ref[...]
    lam = (jnp.exp(jnp.sum(lam_p[0:1] * lam_p[1:2], axis=1, keepdims=True))
           - jnp.exp(jnp.sum(lam_p[2:3] * lam_p[3:4], axis=1, keepdims=True)) + lam_init)
    o = acc0 / l0 - lam * (acc1 / l1)
    ms_o = jnp.mean(o * o, axis=-1, keepdims=True)
    o = o * lax.rsqrt(ms_o + EPS) * subg_ref[...]
    o_ref[...] = (o * (1.0 - lam_init)).astype(BF16)


def _attn_kernel(bounded_ref, lam_ref, subg_ref, q_ref, k_ref, v_ref, o_ref,
                 q2_ref, acc_ref, m_ref, l_ref, ls_ref, *, seq_len, tk, lam_init):
    tq = q_ref.shape[0]
    q = q_ref[...]
    lane = lax.broadcasted_iota(jnp.int32, q.shape, 1)
    zero = jnp.zeros_like(q)
    q2_ref[0:tq, :] = jnp.where(lane < HEAD_DIM, q, zero)
    q2_ref[tq:2 * tq, :] = jnp.where(lane >= HEAD_DIM, q, zero)
    acc_ref[...] = jnp.zeros_like(acc_ref)
    nt = (((1,), (1,)), ((), ()))

    def scores(j):
        start = pl.multiple_of(j * tk, tk)
        s = lax.dot_general(q2_ref[...], k_ref[pl.ds(start, tk), :], nt,
                            preferred_element_type=F32)
        return s, v_ref[pl.ds(start, tk), :]

    def finalize(l):
        acc = acc_ref[...]
        _attn_finalize(lam_ref, subg_ref, o_ref, acc[0:tq], acc[tq:2 * tq],
                       l[0:tq], l[tq:2 * tq], lam_init)

    @pl.when(bounded_ref[0] == 1)
    def _():
        ls_ref[...] = jnp.zeros_like(ls_ref)

        def body(j, carry):
            s, vt = scores(j)
            p = jnp.exp(s)
            part = p[:, 0:LANES]
            for t in range(1, tk // LANES):
                part = part + p[:, t * LANES:(t + 1) * LANES]
            ls_ref[...] += part
            acc_ref[...] += jnp.dot(p.astype(BF16), vt, preferred_element_type=F32)
            return carry

        lax.fori_loop(0, seq_len // tk, body, 0, unroll=ATTN_UNROLL)
        finalize(jnp.sum(ls_ref[...], axis=1, keepdims=True))

    @pl.when(bounded_ref[0] == 0)
    def _():
        m_ref[...] = jnp.full_like(m_ref, -jnp.inf)
        l_ref[...] = jnp.zeros_like(l_ref)

        def body(j, carry):
            s, vt = scores(j)
            m_prev = m_ref[...]
            m_new = jnp.maximum(m_prev, jnp.max(s, axis=1, keepdims=True))
            alpha = jnp.exp(m_prev - m_new)
            p = jnp.exp(s - m_new)
            l_ref[...] = alpha * l_ref[...] + jnp.sum(p, axis=1, keepdims=True)
            acc_ref[...] = alpha * acc_ref[...] + jnp.dot(p.astype(BF16), vt,
                                                          preferred_element_type=F32)
            m_ref[...] = m_new
            return carry

        lax.fori_loop(0, seq_len // tk, body, 0)
        finalize(l_ref[...])


def _attention(q, k, v, bounded, lam_p, sub_g, *, row0, n_seq, seq_len, lam_init, tq, tk):
    nq = seq_len // tq
    q0 = row0 // tq
    kv0 = row0 // seq_len
    kernel = functools.partial(_attn_kernel, seq_len=seq_len, tk=tk, lam_init=lam_init)
    grid_spec = pltpu.PrefetchScalarGridSpec(
        num_scalar_prefetch=1,
        grid=(n_seq, N_HEADS, nq),
        in_specs=[
            pl.BlockSpec((4, HEAD_DIM), lambda b, h, i, f: (0, 0)),
            pl.BlockSpec((1, V_DIM), lambda b, h, i, f: (0, 0)),
            pl.BlockSpec((tq, LANES), lambda b, h, i, f: (q0 + b * nq + i, h)),
            pl.BlockSpec((seq_len, LANES), lambda b, h, i, f: (kv0 + b, h)),
            pl.BlockSpec((seq_len, LANES), lambda b, h, i, f: (kv0 + b, h)),
        ],
        out_specs=pl.BlockSpec((tq, V_DIM), lambda b, h, i, f: (b * nq + i, h)),
        scratch_shapes=[pltpu.VMEM((2 * tq, LANES), BF16), pltpu.VMEM((2 * tq, V_DIM), F32),
                        pltpu.VMEM((2 * tq, 1), F32), pltpu.VMEM((2 * tq, 1), F32),
                        pltpu.VMEM((2 * tq, LANES), F32)],
    )
    return pl.pallas_call(
        kernel,
        grid_spec=grid_spec,
        out_shape=jax.ShapeDtypeStruct((n_seq * seq_len, N_HEADS * V_DIM), BF16),
        compiler_params=_cparams(("parallel", "parallel", "arbitrary")),
        name="diff_attn",
    )(bounded, lam_p, sub_g, q, k, v)


def _outproj_kernel(fp_ref, fs_ref, ap_ref, as_ref, xp_ref, xs_ref, wo_ref, g2_ref, wr_ref, br_ref,
                    x1_ref, h2_ref, idx_ref, gexp_ref, *, n_p):
    acc = (jnp.dot(_pick(n_p, fp_ref, fs_ref).astype(BF16), wo_ref[0:F_WIDTH, :],
                   preferred_element_type=F32)
           + jnp.dot(_pick(n_p, ap_ref, as_ref), wo_ref[F_WIDTH:D_MODEL, :],
                     preferred_element_type=F32))
    x1 = _pick(n_p, xp_ref, xs_ref) + acc
    x1_ref[...] = x1
    ms = jnp.mean(x1 * x1, axis=-1, keepdims=True)
    h2 = (x1 * lax.rsqrt(ms + EPS) * g2_ref[...]).astype(BF16)
    lo = lax.bitcast_convert_type(h2[:, 0:HALF_D].astype(F32), jnp.uint32)
    hi = lax.bitcast_convert_type(h2[:, HALF_D:D_MODEL].astype(F32), jnp.uint32)
    h2_ref[...] = lax.shift_right_logical(lo, jnp.uint32(16)) | (hi & jnp.uint32(0xFFFF0000))
    logits = jnp.dot(h2, wr_ref[...], preferred_element_type=F32) + br_ref[...]
    tm = logits.shape[0]
    lane = lax.broadcasted_iota(jnp.int32, (tm, N_EXPERTS), 1)
    cur = logits
    vals, idxs = [], []
    for _ in range(TOP_K):
        m = jnp.max(cur, axis=1, keepdims=True)
        am = jnp.min(jnp.where(cur == m, lane, N_EXPERTS), axis=1, keepdims=True)
        vals.append(m)
        idxs.append(am)
        cur = jnp.where(lane == am, -jnp.inf, cur)
    idx_ref[...] = jnp.concatenate(idxs, axis=1)
    es = [jnp.exp(v - vals[0]) for v in vals]
    den = es[0] + es[1] + es[2] + es[3]
    gexp_ref[...] = jnp.concatenate([jnp.broadcast_to(e / den, (tm, LANES)) for e in es], axis=1)


def _outproj(f_pair, a_pair, x_pair, w_out_b, g2, w_r_b, b_r, *, tm):
    T = x_pair[0].shape[0] + x_pair[1].shape[0]
    n_p = x_pair[0].shape[0] // tm
    const = lambda i: (0, 0)
    return pl.pallas_call(
        functools.partial(_outproj_kernel, n_p=n_p),
        grid=(T // tm,),
        in_specs=_pair_specs(tm, F_WIDTH, n_p) + _pair_specs(tm, F_WIDTH, n_p)
        + _pair_specs(tm, D_MODEL, n_p) + [
            pl.BlockSpec((D_MODEL, D_MODEL), const),
            pl.BlockSpec((1, D_MODEL), const),
            pl.BlockSpec((D_MODEL, N_EXPERTS), const),
            pl.BlockSpec((1, N_EXPERTS), const),
        ],
        out_specs=[
            pl.BlockSpec((tm, D_MODEL), lambda i: (i, 0)),
            pl.BlockSpec((tm, HALF_D), lambda i: (i, 0)),
            pl.BlockSpec((tm, TOP_K), lambda i: (i, 0)),
            pl.BlockSpec((tm, TOP_K * LANES), lambda i: (i, 0)),
        ],
        out_shape=[
            jax.ShapeDtypeStruct((T, D_MODEL), F32),
            jax.ShapeDtypeStruct((T, HALF_D), jnp.uint32),
            jax.ShapeDtypeStruct((T, TOP_K), jnp.int32),
            jax.ShapeDtypeStruct((T, TOP_K * LANES), F32),
        ],
        compiler_params=_cparams(("parallel",)),
        name="outproj_router",
    )(*f_pair, *a_pair, *x_pair, w_out_b, g2, w_r_b, b_r)


def _route(top_idx, n_blk):
    flat_e = top_idx.reshape(-1)
    onehot = (flat_e[:, None] == jnp.arange(N_EXPERTS, dtype=jnp.int32)[None, :]).astype(jnp.int32)
    csum = jnp.cumsum(onehot, axis=0)
    counts = csum[-1]
    rank = jnp.sum(csum * onehot, axis=1) - 1
    nblk_e = (counts + MOE_TM - 1) // MOE_TM
    blk_end = jnp.cumsum(nblk_e)
    blk_start = blk_end - nblk_e
    dest = jnp.sum(onehot * (blk_start * MOE_TM)[None, :], axis=1) + rank
    b = jnp.arange(n_blk, dtype=jnp.int32)
    n_used = blk_end[-1]
    blk_src = jnp.minimum(b, n_used - 1)
    blk_e = jnp.minimum(jnp.sum((blk_end[None, :] <= blk_src[:, None]).astype(jnp.int32), axis=1),
                        N_EXPERTS - 1)
    blk_used = (b < n_used).astype(jnp.int32)
    pad_start = (blk_start * MOE_TM + counts) // 8 * 8
    pad_len = blk_end * MOE_TM - pad_start
    fill = jnp.concatenate([pad_start, pad_len, n_used[None]]).astype(jnp.int32)
    return dest.astype(jnp.int32), blk_e.astype(jnp.int32), blk_src.astype(jnp.int32), blk_used, fill


def _w1_split_kernel(w_ref, perm_ref, g_ref, l_ref):
    perm = perm_ref[...]
    half = perm.shape[0] // 2
    for c in range(w_ref.shape[2] // perm.shape[0]):
        w = w_ref[0, :, c * 2 * half:(c + 1) * 2 * half].astype(BF16)
        r = jnp.dot(w, perm, preferred_element_type=F32)
        g_ref[0, :, c * half:(c + 1) * half] = r[:, :half].astype(BF16)
        l_ref[0, :, c * half:(c + 1) * half] = r[:, half:].astype(BF16)


def _w1_split(w1):
    n_e, d, f2 = w1.shape
    chunk = 2 * LANES
    tcol = 512
    i = jnp.arange(chunk, dtype=jnp.int32)
    src = jnp.where(i < LANES, 2 * i, 2 * (i - LANES) + 1)
    perm = (i[:, None] == src[None, :]).astype(BF16)
    out = jax.ShapeDtypeStruct((n_e, d, f2 // 2), BF16)
    return pl.pallas_call(
        _w1_split_kernel,
        grid=(n_e, f2 // tcol),
        in_specs=[pl.BlockSpec((1, d, tcol), lambda e, j: (e, 0, j)),
                  pl.BlockSpec((chunk, chunk), lambda e, j: (0, 0))],
        out_specs=[pl.BlockSpec((1, d, tcol // 2), lambda e, j: (e, 0, j))] * 2,
        out_shape=[out, out],
        compiler_params=_cparams(("parallel", "parallel")),
        name="w1_split",
    )(w1, perm)


def _scatter_kernel(fill_ref, dest_ref, h_ref, xs_ref, idx_smem, zero_ref, idx_sem, sem, zero_sem,
                    *, te, n_blk):
    @pl.when(pl.program_id(0) == 0)
    def _():
        zero_ref[...] = jnp.zeros_like(zero_ref)

        def fill_copy(row, size):
            return pltpu.make_async_copy(zero_ref.at[pl.ds(0, size), :],
                                         xs_ref.at[pl.ds(row, size), :], zero_sem)

        def expert_fills(act):
            for e in range(N_EXPERTS):
                row = fill_ref[e]
                for size in FILL_SIZES:
                    take = (fill_ref[N_EXPERTS + e] & size) != 0

                    @pl.when(take)
                    def _(row=row, size=size):
                        act(fill_copy(pl.multiple_of(row, 8), size))

                    row = row + jnp.where(take, size, 0)

        def block_fills(act):
            def one(b, carry):
                act(fill_copy(pl.multiple_of(b * MOE_TM, MOE_TM), MOE_TM))
                return carry

            lax.fori_loop(fill_ref[2 * N_EXPERTS], n_blk, one, 0)

        expert_fills(lambda cp: cp.start())
        block_fills(lambda cp: cp.start())
        expert_fills(lambda cp: cp.wait())
        block_fills(lambda cp: cp.wait())

    cp = pltpu.make_async_copy(dest_ref.at[0, 0], idx_smem, idx_sem)
    cp.start()
    cp.wait()

    def row_copy(t, d):
        return pltpu.make_async_copy(h_ref.at[pl.ds(t, 1), :], xs_ref.at[pl.ds(d, 1), :], sem)

    def issue(t, carry):
        for k in range(TOP_K):
            row_copy(t, idx_smem[TOP_K * t + k]).start()
        return carry

    lax.fori_loop(0, te, issue, 0)
    for k in range(TOP_K):
        pltpu.make_async_copy(h_ref, xs_ref.at[pl.ds(0, te), :], sem).wait()


def _moe_scatter(h2p, dest, fill, n_rows, *, te):
    T = h2p.shape[0]
    dest3 = dest.reshape(T // te, 1, te * TOP_K)
    n_blk = n_rows // MOE_TM
    grid_spec = pltpu.PrefetchScalarGridSpec(
        num_scalar_prefetch=1,
        grid=(T // te,),
        in_specs=[
            pl.BlockSpec((1, 1, te * TOP_K), lambda i, fl: (i, 0, 0)),
            pl.BlockSpec((te, HALF_D), lambda i, fl: (i, 0)),
        ],
        out_specs=pl.BlockSpec(memory_space=pl.ANY),
        scratch_shapes=[pltpu.SMEM((te * TOP_K,), jnp.int32),
                        pltpu.VMEM((MOE_TM, HALF_D), jnp.uint32),
                        pltpu.SemaphoreType.DMA(()), pltpu.SemaphoreType.DMA(()),
                        pltpu.SemaphoreType.DMA(())],
    )
    return pl.pallas_call(
        functools.partial(_scatter_kernel, te=te, n_blk=n_blk),
        grid_spec=grid_spec,
        out_shape=jax.ShapeDtypeStruct((n_rows, HALF_D), jnp.uint32),
        compiler_params=_cparams(("arbitrary",)),
        name="moe_scatter",
    )(fill, dest3, h2p)


def _moe_up_kernel(be_ref, bsrc_ref, bused_ref, x_ref, w1g_ref, w1l_ref, b1g_ref, b1l_ref,
                   o_ref, xb_ref):
    b = pl.program_id(0)

    @pl.when(bused_ref[b] > 0)
    def _():
        w = x_ref[...]
        lo = lax.bitcast_convert_type(lax.shift_left(w, jnp.uint32(16)), F32)
        hi = lax.bitcast_convert_type(w & jnp.uint32(0xFFFF0000), F32)
        xb_ref[:, 0:HALF_D] = lo.astype(BF16)
        xb_ref[:, HALF_D:D_MODEL] = hi.astype(BF16)
        x = xb_ref[...]
        for c in range(D_FF // MOE_TF):
            sl = slice(c * MOE_TF, (c + 1) * MOE_TF)
            hg = jnp.dot(x, w1g_ref[0, :, sl], preferred_element_type=F32) + b1g_ref[0, :, sl]
            hl = jnp.dot(x, w1l_ref[0, :, sl], preferred_element_type=F32) + b1l_ref[0, :, sl]
            hg = jnp.minimum(hg, SWIGLU_LIMIT)
            hl = jnp.clip(hl, -SWIGLU_LIMIT, SWIGLU_LIMIT)
            o_ref[:, sl] = (hg * jax.nn.sigmoid(SWIGLU_ALPHA * hg) * (hl + 1.0)).astype(BF16)

    @pl.when(bused_ref[b] == 0)
    def _():
        o_ref[...] = jnp.zeros_like(o_ref)


def _moe_down_kernel(be_ref, bsrc_ref, bused_ref, a_ref, w2_ref, b2_ref, o_ref):
    b = pl.program_id(0)

    @pl.when(bused_ref[b] > 0)
    def _():
        o_ref[...] = jnp.dot(a_ref[...], w2_ref[0], preferred_element_type=F32) + b2_ref[0]

    @pl.when(bused_ref[b] == 0)
    def _():
        o_ref[...] = jnp.zeros_like(o_ref)


def _moe_ffn(xs, blk_e, blk_src, blk_used, w1g, w1l, b1g, b1l, w2b, b2):
    n_rows = xs.shape[0]
    n_blk = n_rows // MOE_TM
    up_spec = pltpu.PrefetchScalarGridSpec(
        num_scalar_prefetch=3,
        grid=(n_blk,),
        in_specs=[
            pl.BlockSpec((MOE_TM, HALF_D), lambda b, be, bs, bu: (bs[b], 0)),
            pl.BlockSpec((1, D_MODEL, D_FF), lambda b, be, bs, bu: (be[b], 0, 0)),
            pl.BlockSpec((1, D_MODEL, D_FF), lambda b, be, bs, bu: (be[b], 0, 0)),
            pl.BlockSpec((1, 1, D_FF), lambda b, be, bs, bu: (be[b], 0, 0)),
            pl.BlockSpec((1, 1, D_FF), lambda b, be, bs, bu: (be[b], 0, 0)),
        ],
        out_specs=pl.BlockSpec((MOE_TM, D_FF), lambda b, be, bs, bu: (b, 0)),
        scratch_shapes=[pltpu.VMEM((MOE_TM, D_MODEL), BF16)],
    )
    act = pl.pallas_call(
        _moe_up_kernel,
        grid_spec=up_spec,
        out_shape=jax.ShapeDtypeStruct((n_rows, D_FF), BF16),
        compiler_params=_cparams(("arbitrary",)),
        name="moe_up",
    )(blk_e, blk_src, blk_used, xs, w1g, w1l, b1g, b1l)
    down_spec = pltpu.PrefetchScalarGridSpec(
        num_scalar_prefetch=3,
        grid=(n_blk,),
        in_specs=[
            pl.BlockSpec((MOE_TM, D_FF), lambda b, be, bs, bu: (bs[b], 0)),
            pl.BlockSpec((1, D_FF, D_MODEL), lambda b, be, bs, bu: (be[b], 0, 0)),
            pl.BlockSpec((1, 1, D_MODEL), lambda b, be, bs, bu: (be[b], 0, 0)),
        ],
        out_specs=pl.BlockSpec((MOE_TM, D_MODEL), lambda b, be, bs, bu: (b, 0)),
    )
    return pl.pallas_call(
        _moe_down_kernel,
        grid_spec=down_spec,
        out_shape=jax.ShapeDtypeStruct((n_rows, D_MODEL), F32),
        compiler_params=_cparams(("arbitrary",)),
        name="moe_down",
    )(blk_e, blk_src, blk_used, act, w2b, b2)


def _combine_kernel(dest_ref, x1_ref, gexp_ref, ys_ref, o_ref, idx_smem, gbuf, idx_sem, sem, *, tc):
    cp = pltpu.make_async_copy(dest_ref.at[0, 0], idx_smem, idx_sem)
    cp.start()
    cp.wait()

    def row_copy(d, r):
        return pltpu.make_async_copy(ys_ref.at[pl.ds(d, 1), :], gbuf.at[pl.ds(r, 1), :], sem)

    def issue(t, carry):
        for k in range(TOP_K):
            row_copy(idx_smem[TOP_K * t + k], k * tc + t).start()
        return carry

    lax.fori_loop(0, tc, issue, 0)
    pltpu.make_async_copy(ys_ref.at[pl.ds(0, TOP_K * tc), :], gbuf, sem).wait()

    acc = x1_ref[...]
    for k in range(TOP_K):
        gate = jnp.concatenate([gexp_ref[:, k * LANES:(k + 1) * LANES]] * (D_MODEL // LANES), axis=1)
        acc = acc + gate * gbuf[pl.ds(k * tc, tc), :]
    o_ref[...] = acc


def _moe_combine(x1, gexp, dest, ys, *, row0, n_rows, tc):
    T = x1.shape[0]
    i0 = row0 // tc
    return pl.pallas_call(
        functools.partial(_combine_kernel, tc=tc),
        grid=(n_rows // tc,),
        in_specs=[
            pl.BlockSpec((1, 1, tc * TOP_K), lambda i: (i0 + i, 0, 0)),
            pl.BlockSpec((tc, D_MODEL), lambda i: (i0 + i, 0)),
            pl.BlockSpec((tc, TOP_K * LANES), lambda i: (i0 + i, 0)),
            pl.BlockSpec(memory_space=pl.ANY),
        ],
        out_specs=pl.BlockSpec((tc, D_MODEL), lambda i: (i, 0)),
        out_shape=jax.ShapeDtypeStruct((n_rows, D_MODEL), F32),
        scratch_shapes=[pltpu.SMEM((tc * TOP_K,), jnp.int32),
                        pltpu.VMEM((tc * TOP_K, D_MODEL), F32),
                        pltpu.SemaphoreType.DMA(()), pltpu.SemaphoreType.DMA(())],
        compiler_params=_cparams(("arbitrary",)),
        name="moe_combine",
    )(dest.reshape(T // tc, 1, tc * TOP_K), x1, gexp, ys)


@functools.lru_cache(maxsize=None)
def _rope_tables_np(n_pos):
    inv = (np.float32(1.0) / np.power(np.float32(ROPE_THETA),
                                      np.arange(0, HEAD_DIM, 2, dtype=np.float32) / np.float32(HEAD_DIM)))
    ang = np.arange(n_pos, dtype=np.float32)[:, None] * inv.astype(np.float32)[None, :]
    ang = ang.astype(np.float32).astype(np.float64)
    return np.cos(ang).astype(np.float32), np.sin(ang).astype(np.float32)


def _rope_tables(n_pos):
    c, s = _rope_tables_np(n_pos)
    return jnp.tile(jnp.asarray(c), (1, 4)), jnp.tile(jnp.asarray(s), (1, 4))


def _layer(x_pair, l, s_p, n_smp, s_s, norm1_g, w_in, q_norm_g, k_norm_g, lambda_q1, lambda_k1,
           lambda_q2, lambda_k2, sub_norm_g, w_fourier, b_fourier, w_out, norm2_g,
           w_router, b_router, w1, b1, w2, b2):
    T = x_pair[0].shape[0] + x_pair[1].shape[0]
    lam_init = 0.8 - 0.6 * math.exp(-0.3 * l)
    tm_in = min(256, s_s)

    cos, sin = _rope_tables(max(s_p, s_s))
    u, q, k, v = _inproj(
        *x_pair, norm1_g.reshape(1, D_MODEL), w_in.astype(BF16), cos, sin,
        jnp.tile(q_norm_g, 2).reshape(1, LANES), jnp.tile(k_norm_g, 2).reshape(1, LANES),
        tm=tm_in, n_p=s_p // tm_in, n_s=s_s // tm_in)

    w_f_b = w_fourier.astype(BF16)
    u3 = u.reshape(T // DFT_N2, DFT_N2, F_WIDTH)
    f_pair = (_fourier(u3, w_f_b, b_fourier, row0=0, n_seq=1, seq_len=s_p),
              _fourier(u3, w_f_b, b_fourier, row0=s_p, n_seq=n_smp, seq_len=s_s))

    lam_p = jnp.stack([lambda_q1, lambda_k1, lambda_q2, lambda_k2]).astype(F32)
    sub_g = sub_norm_g.reshape(1, V_DIM)
    score_bound = (1.01 * math.sqrt(HEAD_DIM)) * jnp.max(jnp.abs(q_norm_g)) * jnp.max(jnp.abs(k_norm_g))
    bounded = (score_bound <= SCORE_BOUND_NO_MAX).astype(jnp.int32).reshape(1)
    a_pair = (_attention(q, k, v, bounded, lam_p, sub_g, row0=0, n_seq=1, seq_len=s_p,
                         lam_init=lam_init, tq=min(ATTN_TQ, s_p), tk=min(ATTN_TK, s_p)),
              _attention(q, k, v, bounded, lam_p, sub_g, row0=s_p, n_seq=n_smp, seq_len=s_s,
                         lam_init=lam_init, tq=min(ATTN_TQ, s_s), tk=min(ATTN_TK, s_s)))

    x1, h2, top_idx, gexp = _outproj(
        f_pair, a_pair, x_pair, w_out.astype(BF16), norm2_g.reshape(1, D_MODEL), w_router.astype(BF16),
        b_router.reshape(1, N_EXPERTS), tm=min(256, s_s))

    n_blk = T * TOP_K // MOE_TM + N_EXPERTS
    dest, blk_e, blk_src, blk_used, fill = _route(top_idx, n_blk)
    xs = _moe_scatter(h2, dest, fill, n_blk * MOE_TM, te=min(256, s_s))
    w1g, w1l = _w1_split(w1)
    ys = _moe_ffn(
        xs, blk_e, blk_src, blk_used, w1g, w1l,
        b1[:, 0::2].reshape(N_EXPERTS, 1, D_FF), b1[:, 1::2].reshape(N_EXPERTS, 1, D_FF),
        w2.astype(BF16), b2.reshape(N_EXPERTS, 1, D_MODEL))
    tc = min(128, s_s)
    return (_moe_combine(x1, gexp, dest, ys, row0=0, n_rows=s_p, tc=tc),
            _moe_combine(x1, gexp, dest, ys, row0=s_p, n_rows=T - s_p, tc=tc))


def kernel(x_prompt, x_sample, norm1_g, w_in, q_norm_g, k_norm_g, lambda_q1, lambda_k1, lambda_q2,
           lambda_k2, sub_norm_g, w_fourier, b_fourier, w_out, norm2_g, w_router, b_router,
           w1, b1, w2, b2):
    b_p, s_p, d = x_prompt.shape
    n_smp, s_s, _ = x_sample.shape
    assert b_p == 1 and d == D_MODEL
    x_pair = (x_prompt.reshape(s_p, d), x_sample.reshape(n_smp * s_s, d))
    for l in range(norm1_g.shape[0]):
        x_pair = _layer(x_pair, l, s_p, n_smp, s_s, norm1_g[l], w_in[l], q_norm_g[l], k_norm_g[l],
                        lambda_q1[l], lambda_k1[l], lambda_q2[l], lambda_k2[l], sub_norm_g[l],
                        w_fourier[l], b_fourier[l], w_out[l], norm2_g[l], w_router[l], b_router[l],
                        w1[l], b1[l], w2[l], b2[l])
    return (x_pair[0].reshape(1, s_p, d), x_pair[1].reshape(n_smp, s_s, d))
```

```python
import functools
import math

import jax
import jax.numpy as jnp
import numpy as np
from jax import lax
from jax.experimental import pallas as pl
from jax.experimental.pallas import tpu as pltpu

F32 = jnp.float32
BF16 = jnp.bfloat16

D_MODEL = 2048
F_WIDTH = 1024
N_FG = 8
FG_DIM = 128
N_HEADS = 8
HEAD_DIM = 64
V_DIM = 128
QK_WIDTH = 1024
N_EXPERTS = 32
TOP_K = 4
D_FF = 2048
ROPE_THETA = 10000.0
SWIGLU_ALPHA = 1.702
SWIGLU_LIMIT = 7.0
EPS = 1e-5

LANES = 128
HALF_D = D_MODEL // 2
DFT_N2 = 128
VMEM_LIMIT = 48 * 1024 * 1024

MOE_TM = 512
MOE_TF = 512
FILL_SIZES = tuple(MOE_TM >> i for i in range(7))
ATTN_TQ = 512
ATTN_TK = 512
ATTN_UNROLL = 16
SCORE_BOUND_NO_MAX = 40.0


def _cparams(sem):
    return pltpu.CompilerParams(dimension_semantics=sem, vmem_limit_bytes=VMEM_LIMIT)


def _norm_rope_store(acc, g_ref, cos_ref, sin_ref, o_ref, scale):
    tm = acc.shape[0]
    w = 2 * LANES
    r = lax.broadcasted_iota(jnp.int32, (w, w), 0) // HEAD_DIM
    c = lax.broadcasted_iota(jnp.int32, (w, w), 1) // HEAD_DIM
    group_mean = jnp.where(r == c, 1.0 / HEAD_DIM, 0.0).astype(BF16)
    lane = lax.broadcasted_iota(jnp.int32, (tm, w), 1)
    first_half = (lane % HEAD_DIM) < (HEAD_DIM // 2)
    cos = jnp.concatenate([cos_ref[...]] * 2, axis=1)
    sin = jnp.concatenate([sin_ref[...]] * 2, axis=1)
    g = jnp.concatenate([g_ref[...]] * 2, axis=1)
    for h in range(acc.shape[1] // w):
        a = acc[:, h * w:(h + 1) * w]
        sq = a * a
        hi = sq.astype(BF16)
        lo = (sq - hi.astype(F32)).astype(BF16)
        ms = (jnp.dot(hi, group_mean, preferred_element_type=F32)
              + jnp.dot(lo, group_mean, preferred_element_type=F32))
        y = a * lax.rsqrt(ms + EPS) * g
        rot = jnp.where(first_half, -pltpu.roll(y, w - HEAD_DIM // 2, 1),
                        pltpu.roll(y, HEAD_DIM // 2, 1))
        o_ref[:, h * w:(h + 1) * w] = ((y * cos + rot * sin) * scale).astype(BF16)


def _pick(n_first, first_ref, second_ref):
    return jnp.where(pl.program_id(0) < n_first, first_ref[...], second_ref[...])


def _pair_specs(tm, width, n_first):
    return [pl.BlockSpec((tm, width), lambda i: (jnp.minimum(i, n_first - 1), 0)),
            pl.BlockSpec((tm, width), lambda i: (jnp.maximum(i - n_first, 0), 0))]


def _inproj_kernel(xp_ref, xs_ref, g1_ref, w_ref, cos_ref, sin_ref, gq_ref, gk_ref,
                   u_ref, q_ref, k_ref, v_ref, *, n_p):
    x = _pick(n_p, xp_ref, xs_ref)
    ms = jnp.mean(x * x, axis=-1, keepdims=True)
    h = (x * lax.rsqrt(ms + EPS) * g1_ref[...]).astype(BF16)

    def proj(j):
        return jnp.dot(h, w_ref[:, j * QK_WIDTH:(j + 1) * QK_WIDTH], preferred_element_type=F32)

    u_ref[...] = proj(0)
    _norm_rope_store(proj(1), gq_ref, cos_ref, sin_ref, q_ref, 1.0 / math.sqrt(HEAD_DIM))
    _norm_rope_store(proj(2), gk_ref, cos_ref, sin_ref, k_ref, 1.0)
    v_ref[...] = proj(3).astype(BF16)


def _inproj(xp, xs, g1, w_in_b, cos, sin, gq, gk, *, tm, n_p, n_s):
    T = xp.shape[0] + xs.shape[0]

    def pos_blk(i):
        return (jnp.where(i < n_p, i, (i - n_p) % n_s), 0)

    const = lambda i: (0, 0)
    out = jax.ShapeDtypeStruct((T, QK_WIDTH), BF16)
    out_u = jax.ShapeDtypeStruct((T, F_WIDTH), F32)
    return pl.pallas_call(
        functools.partial(_inproj_kernel, n_p=n_p),
        grid=(T // tm,),
        in_specs=_pair_specs(tm, D_MODEL, n_p) + [
            pl.BlockSpec((1, D_MODEL), const),
            pl.BlockSpec((D_MODEL, 4 * QK_WIDTH), const),
            pl.BlockSpec((tm, LANES), pos_blk),
            pl.BlockSpec((tm, LANES), pos_blk),
            pl.BlockSpec((1, LANES), const),
            pl.BlockSpec((1, LANES), const),
        ],
        out_specs=[pl.BlockSpec((tm, QK_WIDTH), lambda i: (i, 0))] * 4,
        out_shape=[out_u, out, out, out],
        compiler_params=_cparams(("parallel",)),
        name="inproj",
    )(xp, xs, g1, w_in_b, cos, sin, gq, gk)


def _fft1_kernel(x_ref, c1_ref, s1_ref, twc_ref, tws_ref, tr_ref, ti_ref):
    c1 = c1_ref[...]
    s1 = s1_ref[...]
    for t in range(x_ref.shape[1]):
        x = x_ref[:, t, :].astype(BF16)
        a = jnp.dot(c1, x, preferred_element_type=F32)
        b = -jnp.dot(s1, x, preferred_element_type=F32)
        c = jnp.concatenate([twc_ref[t]] * N_FG, axis=1)
        s = jnp.concatenate([tws_ref[t]] * N_FG, axis=1)
        tr_ref[t] = a * c + b * s
        ti_ref[t] = b * c - a * s


def _fft2_kernel(tr_ref, ti_ref, c2x_ref, s2x_ref, cc_ref, sc_ref, wf_ref, bf_ref, o_ref, *, scale):
    n = tr_ref.shape[0] * tr_ref.shape[1]
    tr = tr_ref[...].reshape(n, F_WIDTH).astype(BF16)
    ti = ti_ref[...].reshape(n, F_WIDTH).astype(BF16)
    cc = cc_ref[...]
    sc = sc_ref[...]
    for i in range(tr_ref.shape[1]):
        c2 = c2x_ref[i]
        s2 = s2x_ref[i]
        ur = (jnp.dot(c2, tr, preferred_element_type=F32) + jnp.dot(s2, ti, preferred_element_type=F32))
        ui = (jnp.dot(c2, ti, preferred_element_type=F32) - jnp.dot(s2, tr, preferred_element_type=F32))
        outs = []
        for g in range(N_FG):
            sl = slice(g * FG_DIM, (g + 1) * FG_DIM)
            y = (jnp.dot(ur[:, sl].astype(BF16), cc, preferred_element_type=F32)
                 + jnp.dot(ui[:, sl].astype(BF16), sc, preferred_element_type=F32))
            y = (y * scale).astype(BF16)
            outs.append(jnp.dot(y, wf_ref[g], preferred_element_type=F32) + bf_ref[g])
        o_ref[:, i, :] = jnp.concatenate(outs, axis=1)


@functools.lru_cache(maxsize=None)
def _dft_mats_np(n):
    j = np.arange(n)
    ang = (2.0 * np.pi / n) * ((j[:, None] * j[None, :]) % n)
    return np.cos(ang).astype(np.float32), np.sin(ang).astype(np.float32)


def _dft_mats(n):
    c, s = _dft_mats_np(n)
    return jnp.asarray(c).astype(BF16), jnp.asarray(s).astype(BF16)


@functools.lru_cache(maxsize=None)
def _spread_dft_np(tk1):
    c, s = _dft_mats_np(DFT_N2)
    cx = np.zeros((tk1, DFT_N2, DFT_N2 * tk1), np.float32)
    sx = np.zeros((tk1, DFT_N2, DFT_N2 * tk1), np.float32)
    for i in range(tk1):
        cx[i, :, i::tk1] = c
        sx[i, :, i::tk1] = s
    return cx, sx


@functools.lru_cache(maxsize=None)
def _twiddle_np(n1, seq_len):
    ang = (2.0 * np.pi / seq_len) * (np.arange(DFT_N2)[:, None] * np.arange(n1)[None, :])
    return np.cos(ang).astype(np.float32), np.sin(ang).astype(np.float32)


def _fourier(u3, w_f_b, b_f, *, row0, n_seq, seq_len):
    n1 = seq_len // DFT_N2
    blk0 = row0 // seq_len
    c1, s1 = _dft_mats(n1)
    c2, s2 = _dft_mats(DFT_N2)
    tw_cos, tw_sin = _twiddle_np(n1, seq_len)
    twc = jnp.broadcast_to(jnp.asarray(tw_cos)[:, :, None], (DFT_N2, n1, LANES))
    tws = jnp.broadcast_to(jnp.asarray(tw_sin)[:, :, None], (DFT_N2, n1, LANES))

    tb = 8 if n1 > 32 else 32
    t_shape = jax.ShapeDtypeStruct((n_seq, DFT_N2, n1, F_WIDTH), F32)
    tr, ti = pl.pallas_call(
        _fft1_kernel,
        grid=(n_seq, DFT_N2 // tb),
        in_specs=[
            pl.BlockSpec((n1, tb, F_WIDTH), lambda b, j: (blk0 + b, j, 0)),
            pl.BlockSpec((n1, n1), lambda b, j: (0, 0)),
            pl.BlockSpec((n1, n1), lambda b, j: (0, 0)),
            pl.BlockSpec((tb, n1, LANES), lambda b, j: (j, 0, 0)),
            pl.BlockSpec((tb, n1, LANES), lambda b, j: (j, 0, 0)),
        ],
        out_specs=[pl.BlockSpec((None, tb, n1, F_WIDTH), lambda b, j: (b, j, 0, 0))] * 2,
        out_shape=[t_shape] * 2,
        compiler_params=_cparams(("parallel", "parallel")),
        name="fft_stage1",
    )(u3, c1, s1, twc, tws)

    tk1 = min(8, n1)
    c2x, s2x = (jnp.asarray(m).astype(BF16) for m in _spread_dft_np(tk1))
    scale = 1.0 / math.sqrt(seq_len * FG_DIM)
    const = lambda b, j: (0, 0)
    f = pl.pallas_call(
        functools.partial(_fft2_kernel, scale=scale),
        grid=(n_seq, n1 // tk1),
        in_specs=[
            pl.BlockSpec((None, DFT_N2, tk1, F_WIDTH), lambda b, j: (b, 0, j, 0)),
            pl.BlockSpec((None, DFT_N2, tk1, F_WIDTH), lambda b, j: (b, 0, j, 0)),
            pl.BlockSpec((tk1, DFT_N2, DFT_N2 * tk1), lambda b, j: (0, 0, 0)),
            pl.BlockSpec((tk1, DFT_N2, DFT_N2 * tk1), lambda b, j: (0, 0, 0)),
            pl.BlockSpec((FG_DIM, FG_DIM), const),
            pl.BlockSpec((FG_DIM, FG_DIM), const),
            pl.BlockSpec((N_FG, FG_DIM, FG_DIM), lambda b, j: (0, 0, 0)),
            pl.BlockSpec((N_FG, 1, FG_DIM), lambda b, j: (0, 0, 0)),
        ],
        out_specs=pl.BlockSpec((None, DFT_N2, tk1, F_WIDTH), lambda b, j: (b, 0, j, 0)),
        out_shape=jax.ShapeDtypeStruct((n_seq, DFT_N2, n1, F_WIDTH), F32),
        compiler_params=_cparams(("parallel", "parallel")),
        name="fft_stage2",
    )(tr, ti, c2x, s2x, c2, s2, w_f_b, b_f.reshape(N_FG, 1, FG_DIM))
    return f.reshape(n_seq * seq_len, F_WIDTH)


def _attn_finalize(lam_ref, subg_ref, o_ref, acc0, acc1, l0, l1, lam_init):
    lam_p = lam_ref[...]
    lam = (jnp.exp(jnp.sum(lam_p[0:1] * lam_p[1:2], axis=1, keepdims=True))
           - jnp.exp(jnp.sum(lam_p[2:3] * lam_p[3:4], axis=1, keepdims=True)) + lam_init)
    o = acc0 / l0 - lam * (acc1 / l1)
    ms_o = jnp.mean(o * o, axis=-1, keepdims=True)
    o = o * lax.rsqrt(ms_o + EPS) * subg_ref[...]
    o_ref[...] = (o * (1.0 - lam_init)).astype(BF16)


def _attn_kernel(bounded_ref, lam_ref, subg_ref, q_ref, k_ref, v_ref, o_ref,
                 q2_ref, acc_ref, m_ref, l_ref, ls_ref, *, seq_len, tk, lam_init):
    tq = q_ref.shape[0]
    q = q_ref[...]
    lane = lax.broadcasted_iota(jnp.int32, q.shape, 1)
    zero = jnp.zeros_like(q)
    q2_ref[0:tq, :] = jnp.where(lane < HEAD_DIM, q, zero)
    q2_ref[tq:2 * tq, :] = jnp.where(lane >= HEAD_DIM, q, zero)
    acc_ref[...] = jnp.zeros_like(acc_ref)
    nt = (((1,), (1,)), ((), ()))

    def scores(j):
        start = pl.multiple_of(j * tk, tk)
        s = lax.dot_general(q2_ref[...], k_ref[pl.ds(start, tk), :], nt,
                            preferred_element_type=F32)
        return s, v_ref[pl.ds(start, tk), :]

    def finalize(l):
        acc = acc_ref[...]
        _attn_finalize(lam_ref, subg_ref, o_ref, acc[0:tq], acc[tq:2 * tq],
                       l[0:tq], l[tq:2 * tq], lam_init)

    @pl.when(bounded_ref[0] == 1)
    def _():
        ls_ref[...] = jnp.zeros_like(ls_ref)

        def body(j, carry):
            s, vt = scores(j)
            p = jnp.exp(s)
            part = p[:, 0:LANES]
            for t in range(1, tk // LANES):
                part = part + p[:, t * LANES:(t + 1) * LANES]
            ls_ref[...] += part
            acc_ref[...] += jnp.dot(p.astype(BF16), vt, preferred_element_type=F32)
            return carry

        lax.fori_loop(0, seq_len // tk, body, 0, unroll=ATTN_UNROLL)
        finalize(jnp.sum(ls_ref[...], axis=1, keepdims=True))

    @pl.when(bounded_ref[0] == 0)
    def _():
        m_ref[...] = jnp.full_like(m_ref, -jnp.inf)
        l_ref[...] = jnp.zeros_like(l_ref)

        def body(j, carry):
            s, vt = scores(j)
            m_prev = m_ref[...]
            m_new = jnp.maximum(m_prev, jnp.max(s, axis=1, keepdims=True))
            alpha = jnp.exp(m_prev - m_new)
            p = jnp.exp(s - m_new)
            l_ref[...] = alpha * l_ref[...] + jnp.sum(p, axis=1, keepdims=True)
            acc_ref[...] = alpha * acc_ref[...] + jnp.dot(p.astype(BF16), vt,
                                                          preferred_element_type=F32)
            m_ref[...] = m_new
            return carry

        lax.fori_loop(0, seq_len // tk, body, 0)
        finalize(l_ref[...])


def _attention(q, k, v, bounded, lam_p, sub_g, *, row0, n_seq, seq_len, lam_init, tq, tk):
    nq = seq_len // tq
    q0 = row0 // tq
    kv0 = row0 // seq_len
    kernel = functools.partial(_attn_kernel, seq_len=seq_len, tk=tk, lam_init=lam_init)
    grid_spec = pltpu.PrefetchScalarGridSpec(
        num_scalar_prefetch=1,
        grid=(n_seq, N_HEADS, nq),
        in_specs=[
            pl.BlockSpec((4, HEAD_DIM), lambda b, h, i, f: (0, 0)),
            pl.BlockSpec((1, V_DIM), lambda b, h, i, f: (0, 0)),
            pl.BlockSpec((tq, LANES), lambda b, h, i, f: (q0 + b * nq + i, h)),
            pl.BlockSpec((seq_len, LANES), lambda b, h, i, f: (kv0 + b, h)),
            pl.BlockSpec((seq_len, LANES), lambda b, h, i, f: (kv0 + b, h)),
        ],
        out_specs=pl.BlockSpec((tq, V_DIM), lambda b, h, i, f: (b * nq + i, h)),
        scratch_shapes=[pltpu.VMEM((2 * tq, LANES), BF16), pltpu.VMEM((2 * tq, V_DIM), F32),
                        pltpu.VMEM((2 * tq, 1), F32), pltpu.VMEM((2 * tq, 1), F32),
                        pltpu.VMEM((2 * tq, LANES), F32)],
    )
    return pl.pallas_call(
        kernel,
        grid_spec=grid_spec,
        out_shape=jax.ShapeDtypeStruct((n_seq * seq_len, N_HEADS * V_DIM), BF16),
        compiler_params=_cparams(("parallel", "parallel", "arbitrary")),
        name="diff_attn",
    )(bounded, lam_p, sub_g, q, k, v)


def _outproj_kernel(fp_ref, fs_ref, ap_ref, as_ref, xp_ref, xs_ref, wo_ref, g2_ref, wr_ref, br_ref,
                    x1_ref, h2_ref, idx_ref, gexp_ref, *, n_p):
    acc = (jnp.dot(_pick(n_p, fp_ref, fs_ref).astype(BF16), wo_ref[0:F_WIDTH, :],
                   preferred_element_type=F32)
           + jnp.dot(_pick(n_p, ap_ref, as_ref), wo_ref[F_WIDTH:D_MODEL, :],
                     preferred_element_type=F32))
    x1 = _pick(n_p, xp_ref, xs_ref) + acc
    x1_ref[...] = x1
    ms = jnp.mean(x1 * x1, axis=-1, keepdims=True)
    h2 = (x1 * lax.rsqrt(ms + EPS) * g2_ref[...]).astype(BF16)
    lo = lax.bitcast_convert_type(h2[:, 0:HALF_D].astype(F32), jnp.uint32)
    hi = lax.bitcast_convert_type(h2[:, HALF_D:D_MODEL].astype(F32), jnp.uint32)
    h2_ref[...] = lax.shift_right_logical(lo, jnp.uint32(16)) | (hi & jnp.uint32(0xFFFF0000))
    logits = jnp.dot(h2, wr_ref[...], preferred_element_type=F32) + br_ref[...]
    tm = logits.shape[0]
    lane = lax.broadcasted_iota(jnp.int32, (tm, N_EXPERTS), 1)
    cur = logits
    vals, idxs = [], []
    for _ in range(TOP_K):
        m = jnp.max(cur, axis=1, keepdims=True)
        am = jnp.min(jnp.where(cur == m, lane, N_EXPERTS), axis=1, keepdims=True)
        vals.append(m)
        idxs.append(am)
        cur = jnp.where(lane == am, -jnp.inf, cur)
    idx_ref[...] = jnp.concatenate(idxs, axis=1)
    es = [jnp.exp(v - vals[0]) for v in vals]
    den = es[0] + es[1] + es[2] + es[3]
    gexp_ref[...] = jnp.concatenate([jnp.broadcast_to(e / den, (tm, LANES)) for e in es], axis=1)


def _outproj(f_pair, a_pair, x_pair, w_out_b, g2, w_r_b, b_r, *, tm):
    T = x_pair[0].shape[0] + x_pair[1].shape[0]
    n_p = x_pair[0].shape[0] // tm
    const = lambda i: (0, 0)
    return pl.pallas_call(
        functools.partial(_outproj_kernel, n_p=n_p),
        grid=(T // tm,),
        in_specs=_pair_specs(tm, F_WIDTH, n_p) + _pair_specs(tm, F_WIDTH, n_p)
        + _pair_specs(tm, D_MODEL, n_p) + [
            pl.BlockSpec((D_MODEL, D_MODEL), const),
            pl.BlockSpec((1, D_MODEL), const),
            pl.BlockSpec((D_MODEL, N_EXPERTS), const),
            pl.BlockSpec((1, N_EXPERTS), const),
        ],
        out_specs=[
            pl.BlockSpec((tm, D_MODEL), lambda i: (i, 0)),
            pl.BlockSpec((tm, HALF_D), lambda i: (i, 0)),
            pl.BlockSpec((tm, TOP_K), lambda i: (i, 0)),
            pl.BlockSpec((tm, TOP_K * LANES), lambda i: (i, 0)),
        ],
        out_shape=[
            jax.ShapeDtypeStruct((T, D_MODEL), F32),
            jax.ShapeDtypeStruct((T, HALF_D), jnp.uint32),
            jax.ShapeDtypeStruct((T, TOP_K), jnp.int32),
            jax.ShapeDtypeStruct((T, TOP_K * LANES), F32),
        ],
        compiler_params=_cparams(("parallel",)),
        name="outproj_router",
    )(*f_pair, *a_pair, *x_pair, w_out_b, g2, w_r_b, b_r)


def _route(top_idx, n_blk):
    flat_e = top_idx.reshape(-1)
    onehot = (flat_e[:, None] == jnp.arange(N_EXPERTS, dtype=jnp.int32)[None, :]).astype(jnp.int32)
    csum = jnp.cumsum(onehot, axis=0)
    counts = csum[-1]
    rank = jnp.sum(csum * onehot, axis=1) - 1
    nblk_e = (counts + MOE_TM - 1) // MOE_TM
    blk_end = jnp.cumsum(nblk_e)
    blk_start = blk_end - nblk_e
    dest = jnp.sum(onehot * (blk_start * MOE_TM)[None, :], axis=1) + rank
    b = jnp.arange(n_blk, dtype=jnp.int32)
    n_used = blk_end[-1]
    blk_src = jnp.minimum(b, n_used - 1)
    blk_e = jnp.minimum(jnp.sum((blk_end[None, :] <= blk_src[:, None]).astype(jnp.int32), axis=1),
                        N_EXPERTS - 1)
    blk_used = (b < n_used).astype(jnp.int32)
    pad_start = (blk_start * MOE_TM + counts) // 8 * 8
    pad_len = blk_end * MOE_TM - pad_start
    fill = jnp.concatenate([pad_start, pad_len, n_used[None]]).astype(jnp.int32)
    return dest.astype(jnp.int32), blk_e.astype(jnp.int32), blk_src.astype(jnp.int32), blk_used, fill


def _w1_split_kernel(w_ref, perm_ref, g_ref, l_ref):
    perm = perm_ref[...]
    half = perm.shape[0] // 2
    for c in range(w_ref.shape[2] // perm.shape[0]):
        w = w_ref[0, :, c * 2 * half:(c + 1) * 2 * half].astype(BF16)
        r = jnp.dot(w, perm, preferred_element_type=F32)
        g_ref[0, :, c * half:(c + 1) * half] = r[:, :half].astype(BF16)
        l_ref[0, :, c * half:(c + 1) * half] = r[:, half:].astype(BF16)


def _w1_split(w1):
    n_e, d, f2 = w1.shape
    chunk = 2 * LANES
    tcol = 512
    i = jnp.arange(chunk, dtype=jnp.int32)
    src = jnp.where(i < LANES, 2 * i, 2 * (i - LANES) + 1)
    perm = (i[:, None] == src[None, :]).astype(BF16)
    out = jax.ShapeDtypeStruct((n_e, d, f2 // 2), BF16)
    return pl.pallas_call(
        _w1_split_kernel,
        grid=(n_e, f2 // tcol),
        in_specs=[pl.BlockSpec((1, d, tcol), lambda e, j: (e, 0, j)),
                  pl.BlockSpec((chunk, chunk), lambda e, j: (0, 0))],
        out_specs=[pl.BlockSpec((1, d, tcol // 2), lambda e, j: (e, 0, j))] * 2,
        out_shape=[out, out],
        compiler_params=_cparams(("parallel", "parallel")),
        name="w1_split",
    )(w1, perm)


def _scatter_kernel(fill_ref, dest_ref, h_ref, xs_ref, idx_smem, zero_ref, idx_sem, sem, zero_sem,
                    *, te, n_blk):
    @pl.when(pl.program_id(0) == 0)
    def _():
        zero_ref[...] = jnp.zeros_like(zero_ref)

        def fill_copy(row, size):
            return pltpu.make_async_copy(zero_ref.at[pl.ds(0, size), :],
                                         xs_ref.at[pl.ds(row, size), :], zero_sem)

        def expert_fills(act):
            for e in range(N_EXPERTS):
                row = fill_ref[e]
                for size in FILL_SIZES:
                    take = (fill_ref[N_EXPERTS + e] & size) != 0

                    @pl.when(take)
                    def _(row=row, size=size):
                        act(fill_copy(pl.multiple_of(row, 8), size))

                    row = row + jnp.where(take, size, 0)

        def block_fills(act):
            def one(b, carry):
                act(fill_copy(pl.multiple_of(b * MOE_TM, MOE_TM), MOE_TM))
                return carry

            lax.fori_loop(fill_ref[2 * N_EXPERTS], n_blk, one, 0)

        expert_fills(lambda cp: cp.start())
        block_fills(lambda cp: cp.start())
        expert_fills(lambda cp: cp.wait())
        block_fills(lambda cp: cp.wait())

    cp = pltpu.make_async_copy(dest_ref.at[0, 0], idx_smem, idx_sem)
    cp.start()
    cp.wait()

    def row_copy(t, d):
        return pltpu.make_async_copy(h_ref.at[pl.ds(t, 1), :], xs_ref.at[pl.ds(d, 1), :], sem)

    def issue(t, carry):
        for k in range(TOP_K):
            row_copy(t, idx_smem[TOP_K * t + k]).start()
        return carry

    lax.fori_loop(0, te, issue, 0)
    for k in range(TOP_K):
        pltpu.make_async_copy(h_ref, xs_ref.at[pl.ds(0, te), :], sem).wait()


def _moe_scatter(h2p, dest, fill, n_rows, *, te):
    T = h2p.shape[0]
    dest3 = dest.reshape(T // te, 1, te * TOP_K)
    n_blk = n_rows // MOE_TM
    grid_spec = pltpu.PrefetchScalarGridSpec(
        num_scalar_prefetch=1,
        grid=(T // te,),
        in_specs=[
            pl.BlockSpec((1, 1, te * TOP_K), lambda i, fl: (i, 0, 0)),
            pl.BlockSpec((te, HALF_D), lambda i, fl: (i, 0)),
        ],
        out_specs=pl.BlockSpec(memory_space=pl.ANY),
        scratch_shapes=[pltpu.SMEM((te * TOP_K,), jnp.int32),
                        pltpu.VMEM((MOE_TM, HALF_D), jnp.uint32),
                        pltpu.SemaphoreType.DMA(()), pltpu.SemaphoreType.DMA(()),
                        pltpu.SemaphoreType.DMA(())],
    )
    return pl.pallas_call(
        functools.partial(_scatter_kernel, te=te, n_blk=n_blk),
        grid_spec=grid_spec,
        out_shape=jax.ShapeDtypeStruct((n_rows, HALF_D), jnp.uint32),
        compiler_params=_cparams(("arbitrary",)),
        name="moe_scatter",
    )(fill, dest3, h2p)


def _moe_up_kernel(be_ref, bsrc_ref, bused_ref, x_ref, w1g_ref, w1l_ref, b1g_ref, b1l_ref,
                   o_ref, xb_ref):
    b = pl.program_id(0)

    @pl.when(bused_ref[b] > 0)
    def _():
        w = x_ref[...]
        lo = lax.bitcast_convert_type(lax.shift_left(w, jnp.uint32(16)), F32)
        hi = lax.bitcast_convert_type(w & jnp.uint32(0xFFFF0000), F32)
        xb_ref[:, 0:HALF_D] = lo.astype(BF16)
        xb_ref[:, HALF_D:D_MODEL] = hi.astype(BF16)
        x = xb_ref[...]
        for c in range(D_FF // MOE_TF):
            sl = slice(c * MOE_TF, (c + 1) * MOE_TF)
            hg = jnp.dot(x, w1g_ref[0, :, sl], preferred_element_type=F32) + b1g_ref[0, :, sl]
            hl = jnp.dot(x, w1l_ref[0, :, sl], preferred_element_type=F32) + b1l_ref[0, :, sl]
            hg = jnp.minimum(hg, SWIGLU_LIMIT)
            hl = jnp.clip(hl, -SWIGLU_LIMIT, SWIGLU_LIMIT)
            o_ref[:, sl] = (hg * jax.nn.sigmoid(SWIGLU_ALPHA * hg) * (hl + 1.0)).astype(BF16)

    @pl.when(bused_ref[b] == 0)
    def _():
        o_ref[...] = jnp.zeros_like(o_ref)


def _moe_down_kernel(be_ref, bsrc_ref, bused_ref, a_ref, w2_ref, b2_ref, o_ref):
    b = pl.program_id(0)

    @pl.when(bused_ref[b] > 0)
    def _():
        o_ref[...] = jnp.dot(a_ref[...], w2_ref[0].astype(BF16), preferred_element_type=F32) + b2_ref[0]

    @pl.when(bused_ref[b] == 0)
    def _():
        o_ref[...] = jnp.zeros_like(o_ref)


def _moe_ffn(xs, blk_e, blk_src, blk_used, w1g, w1l, b1g, b1l, w2b, b2):
    n_rows = xs.shape[0]
    n_blk = n_rows // MOE_TM
    up_spec = pltpu.PrefetchScalarGridSpec(
        num_scalar_prefetch=3,
        grid=(n_blk,),
        in_specs=[
            pl.BlockSpec((MOE_TM, HALF_D), lambda b, be, bs, bu: (bs[b], 0)),
            pl.BlockSpec((1, D_MODEL, D_FF), lambda b, be, bs, bu: (be[b], 0, 0)),
            pl.BlockSpec((1, D_MODEL, D_FF), lambda b, be, bs, bu: (be[b], 0, 0)),
            pl.BlockSpec((1, 1, D_FF), lambda b, be, bs, bu: (be[b], 0, 0)),
            pl.BlockSpec((1, 1, D_FF), lambda b, be, bs, bu: (be[b], 0, 0)),
        ],
        out_specs=pl.BlockSpec((MOE_TM, D_FF), lambda b, be, bs, bu: (b, 0)),
        scratch_shapes=[pltpu.VMEM((MOE_TM, D_MODEL), BF16)],
    )
    act = pl.pallas_call(
        _moe_up_kernel,
        grid_spec=up_spec,
        out_shape=jax.ShapeDtypeStruct((n_rows, D_FF), BF16),
        compiler_params=_cparams(("arbitrary",)),
        name="moe_up",
    )(blk_e, blk_src, blk_used, xs, w1g, w1l, b1g, b1l)
    down_spec = pltpu.PrefetchScalarGridSpec(
        num_scalar_prefetch=3,
        grid=(n_blk,),
        in_specs=[
            pl.BlockSpec((MOE_TM, D_FF), lambda b, be, bs, bu: (bs[b], 0)),
            pl.BlockSpec((1, D_FF, D_MODEL), lambda b, be, bs, bu: (be[b], 0, 0)),
            pl.BlockSpec((1, 1, D_MODEL), lambda b, be, bs, bu: (be[b], 0, 0)),
        ],
        out_specs=pl.BlockSpec((MOE_TM, D_MODEL), lambda b, be, bs, bu: (b, 0)),
    )
    return pl.pallas_call(
        _moe_down_kernel,
        grid_spec=down_spec,
        out_shape=jax.ShapeDtypeStruct((n_rows, D_MODEL), F32),
        compiler_params=_cparams(("arbitrary",)),
        name="moe_down",
    )(blk_e, blk_src, blk_used, act, w2b, b2)


def _combine_kernel(dest_ref, x1_ref, gexp_ref, ys_ref, o_ref, idx_smem, gbuf, idx_sem, sem, *, tc):
    cp = pltpu.make_async_copy(dest_ref.at[0, 0], idx_smem, idx_sem)
    cp.start()
    cp.wait()

    def row_copy(d, r):
        return pltpu.make_async_copy(ys_ref.at[pl.ds(d, 1), :], gbuf.at[pl.ds(r, 1), :], sem)

    def issue(t, carry):
        for k in range(TOP_K):
            row_copy(idx_smem[TOP_K * t + k], k * tc + t).start()
        return carry

    lax.fori_loop(0, tc, issue, 0)
    pltpu.make_async_copy(ys_ref.at[pl.ds(0, TOP_K * tc), :], gbuf, sem).wait()

    acc = x1_ref[...]
    for k in range(TOP_K):
        gate = jnp.concatenate([gexp_ref[:, k * LANES:(k + 1) * LANES]] * (D_MODEL // LANES), axis=1)
        acc = acc + gate * gbuf[pl.ds(k * tc, tc), :]
    o_ref[...] = acc


def _moe_combine(x1, gexp, dest, ys, *, row0, n_rows, tc):
    T = x1.shape[0]
    i0 = row0 // tc
    return pl.pallas_call(
        functools.partial(_combine_kernel, tc=tc),
        grid=(n_rows // tc,),
        in_specs=[
            pl.BlockSpec((1, 1, tc * TOP_K), lambda i: (i0 + i, 0, 0)),
            pl.BlockSpec((tc, D_MODEL), lambda i: (i0 + i, 0)),
            pl.BlockSpec((tc, TOP_K * LANES), lambda i: (i0 + i, 0)),
            pl.BlockSpec(memory_space=pl.ANY),
        ],
        out_specs=pl.BlockSpec((tc, D_MODEL), lambda i: (i, 0)),
        out_shape=jax.ShapeDtypeStruct((n_rows, D_MODEL), F32),
        scratch_shapes=[pltpu.SMEM((tc * TOP_K,), jnp.int32),
                        pltpu.VMEM((tc * TOP_K, D_MODEL), F32),
                        pltpu.SemaphoreType.DMA(()), pltpu.SemaphoreType.DMA(())],
        compiler_params=_cparams(("arbitrary",)),
        name="moe_combine",
    )(dest.reshape(T // tc, 1, tc * TOP_K), x1, gexp, ys)


@functools.lru_cache(maxsize=None)
def _rope_tables_np(n_pos):
    inv = (np.float32(1.0) / np.power(np.float32(ROPE_THETA),
                                      np.arange(0, HEAD_DIM, 2, dtype=np.float32) / np.float32(HEAD_DIM)))
    ang = np.arange(n_pos, dtype=np.float32)[:, None] * inv.astype(np.float32)[None, :]
    ang = ang.astype(np.float32).astype(np.float64)
    return np.cos(ang).astype(np.float32), np.sin(ang).astype(np.float32)


def _rope_tables(n_pos):
    c, s = _rope_tables_np(n_pos)
    return jnp.tile(jnp.asarray(c), (1, 4)), jnp.tile(jnp.asarray(s), (1, 4))


def _layer(x_pair, l, s_p, n_smp, s_s, norm1_g, w_in, q_norm_g, k_norm_g, lambda_q1, lambda_k1,
           lambda_q2, lambda_k2, sub_norm_g, w_fourier, b_fourier, w_out, norm2_g,
           w_router, b_router, w1, b1, w2, b2):
    T = x_pair[0].shape[0] + x_pair[1].shape[0]
    lam_init = 0.8 - 0.6 * math.exp(-0.3 * l)
    tm_in = min(256, s_s)

    cos, sin = _rope_tables(max(s_p, s_s))
    u, q, k, v = _inproj(
        *x_pair, norm1_g.reshape(1, D_MODEL), w_in.astype(BF16), cos, sin,
        jnp.tile(q_norm_g, 2).reshape(1, LANES), jnp.tile(k_norm_g, 2).reshape(1, LANES),
        tm=tm_in, n_p=s_p // tm_in, n_s=s_s // tm_in)

    w_f_b = w_fourier.astype(BF16)
    u3 = u.reshape(T // DFT_N2, DFT_N2, F_WIDTH)
    f_pair = (_fourier(u3, w_f_b, b_fourier, row0=0, n_seq=1, seq_len=s_p),
              _fourier(u3, w_f_b, b_fourier, row0=s_p, n_seq=n_smp, seq_len=s_s))

    lam_p = jnp.stack([lambda_q1, lambda_k1, lambda_q2, lambda_k2]).astype(F32)
    sub_g = sub_norm_g.reshape(1, V_DIM)
    score_bound = (1.01 * math.sqrt(HEAD_DIM)) * jnp.max(jnp.abs(q_norm_g)) * jnp.max(jnp.abs(k_norm_g))
    bounded = (score_bound <= SCORE_BOUND_NO_MAX).astype(jnp.int32).reshape(1)
    a_pair = (_attention(q, k, v, bounded, lam_p, sub_g, row0=0, n_seq=1, seq_len=s_p,
                         lam_init=lam_init, tq=min(ATTN_TQ, s_p), tk=min(ATTN_TK, s_p)),
              _attention(q, k, v, bounded, lam_p, sub_g, row0=s_p, n_seq=n_smp, seq_len=s_s,
                         lam_init=lam_init, tq=min(ATTN_TQ, s_s), tk=min(ATTN_TK, s_s)))

    x1, h2, top_idx, gexp = _outproj(
        f_pair, a_pair, x_pair, w_out.astype(BF16), norm2_g.reshape(1, D_MODEL), w_router.astype(BF16),
        b_router.reshape(1, N_EXPERTS), tm=min(256, s_s))

    n_blk = T * TOP_K // MOE_TM + N_EXPERTS
    dest, blk_e, blk_src, blk_used, fill = _route(top_idx, n_blk)
    xs = _moe_scatter(h2, dest, fill, n_blk * MOE_TM, te=min(512, s_s))
    w1g, w1l = _w1_split(w1)
    ys = _moe_ffn(
        xs, blk_e, blk_src, blk_used, w1g, w1l,
        b1[:, 0::2].reshape(N_EXPERTS, 1, D_FF), b1[:, 1::2].reshape(N_EXPERTS, 1, D_FF),
        w2, b2.reshape(N_EXPERTS, 1, D_MODEL))
    tc = min(256, s_s)
    return (_moe_combine(x1, gexp, dest, ys, row0=0, n_rows=s_p, tc=tc),
            _moe_combine(x1, gexp, dest, ys, row0=s_p, n_rows=T - s_p, tc=tc))


def kernel(x_prompt, x_sample, norm1_g, w_in, q_norm_g, k_norm_g, lambda_q1, lambda_k1, lambda_q2,
           lambda_k2, sub_norm_g, w_fourier, b_fourier, w_out, norm2_g, w_router, b_router,
           w1, b1, w2, b2):
    b_p, s_p, d = x_prompt.shape
    n_smp, s_s, _ = x_sample.shape
    assert b_p == 1 and d == D_MODEL
    x_pair = (x_prompt.reshape(s_p, d), x_sample.reshape(n_smp * s_s, d))
    for l in range(norm1_g.shape[0]):
        x_pair = _layer(x_pair, l, s_p, n_smp, s_s, norm1_g[l], w_in[l], q_norm_g[l], k_norm_g[l],
                        lambda_q1[l], lambda_k1[l], lambda_q2[l], lambda_k2[l], sub_norm_g[l],
                        w_fourier[l], b_fourier[l], w_out[l], norm2_g[l], w_router[l], b_router[l],
                        w1[l], b1[l], w2[l], b2[l])
    return (x_pair[0].reshape(1, s_p, d), x_pair[1].reshape(n_smp, s_s, d))
```
